```python
import math
import jax, jax.numpy as jnp
from jax import lax
import numpy as np

D_MODEL = 1024
BATCH = 8
SEQ = 4096
DEPTH = 2

CHUNK = 64
S5_WIDTH = 512
S5_GROUP = 16
S5_GROUPS = S5_WIDTH // S5_GROUP
S5_STATE = 64
CONV_WIDTH = 512
CONV_K = 3
ATTN_HEADS = 8
HEAD_DIM = 64
ATTN_WIDTH = ATTN_HEADS * HEAD_DIM
Q_BLOCK = 128
N_BRANCH = 3
MIX_WIDTH = S5_WIDTH + CONV_WIDTH + ATTN_WIDTH
D_FF = 4 * D_MODEL
IN_WIDTH = S5_WIDTH + 3 * CONV_WIDTH + 3 * ATTN_WIDTH + ATTN_HEADS + N_BRANCH * D_MODEL
EPS = 1e-6
NEG_INF = -1e30

kernel_name = "hybrid_s5_shortconv_fox_gated_encoder"


def rmsnorm(x, g):
    xf = x.astype(jnp.float32)
    y = xf * lax.rsqrt(jnp.mean(xf * xf, axis=-1, keepdims=True) + EPS)
    return (y * g.astype(jnp.float32)).astype(x.dtype)


def s5_mixer(u, a_re, a_im, log_dt, b_re, b_im, c_re, c_im, d_skip, w_glu):
    bsz, L, _ = u.shape
    f32 = jnp.float32
    uf = u.astype(f32).reshape(bsz, L, S5_GROUPS, S5_GROUP)
    lam = lax.complex(jnp.minimum(a_re.astype(f32), -1e-4), a_im.astype(f32))
    dt = jnp.exp(log_dt.astype(f32))[:, None]
    a_bar = jnp.exp(lam * dt)
    b = lax.complex(b_re.astype(f32), b_im.astype(f32))
    b_bar = ((a_bar - 1.0) / lam)[..., None] * b
    bu = lax.complex(jnp.einsum('gph,blgh->blgp', jnp.real(b_bar), uf),
                     jnp.einsum('gph,blgh->blgp', jnp.imag(b_bar), uf))
    a_seq = jnp.broadcast_to(a_bar, bu.shape)

    def combine(left, right):
        a_l, s_l = left
        a_r, s_r = right
        return a_r * a_l, a_r * s_l + s_r

    _, states = lax.associative_scan(combine, (a_seq, bu), axis=1)
    y = (jnp.einsum('ghp,blgp->blgh', c_re.astype(f32), jnp.real(states))
         - jnp.einsum('ghp,blgp->blgh', c_im.astype(f32), jnp.imag(states))
         + d_skip.astype(f32) * uf)
    y = jax.nn.gelu(y)
    y = y * jax.nn.sigmoid(jnp.einsum('blgh,ghk->blgk', y, w_glu.astype(f32)))
    return y.reshape(bsz, L, S5_WIDTH).astype(u.dtype)


def short_conv_mixer(x_in, gate_b, gate_c, conv_w):
    L = x_in.shape[1]
    v = gate_c * x_in
    vp = jnp.pad(v, ((0, 0), (CONV_K - 1, 0), (0, 0)))
    conv = vp[:, 0:L] * conv_w[0]
    for tap in range(1, CONV_K):
        conv = conv + vp[:, tap:tap + L] * conv_w[tap]
    return gate_b * conv


def forgetting_attention(q, k, v, f_logit, b_f):
    bsz, L, _ = q.shape
    q = q.reshape(bsz, L, ATTN_HEADS, HEAD_DIM)
    k = k.reshape(bsz, L, ATTN_HEADS, HEAD_DIM)
    v = v.reshape(bsz, L, ATTN_HEADS, HEAD_DIM)
    log_f = jax.nn.log_sigmoid((f_logit + b_f).astype(jnp.float32))
    cum = jnp.cumsum(log_f, axis=1).transpose(0, 2, 1)
    scale = HEAD_DIM ** -0.5
    outs = []
    for blk in range(L // Q_BLOCK):
        q0 = blk * Q_BLOCK
        kend = q0 + Q_BLOCK
        s = jnp.einsum('bqhd,bkhd->bhqk', q[:, q0:kend], k[:, :kend]).astype(jnp.float32) * scale
        s = s + cum[:, :, q0:kend, None] - cum[:, :, None, :kend]
        q_pos = q0 + jnp.arange(Q_BLOCK)
        k_pos = jnp.arange(kend)
        s = jnp.where(k_pos[None, :] <= q_pos[:, None], s, NEG_INF)
        p = jax.nn.softmax(s, axis=-1)
        outs.append(jnp.einsum('bhqk,bkhd->bqhd', p.astype(v.dtype), v[:, :kend]))
    return jnp.concatenate(outs, axis=1).reshape(bsz, L, ATTN_WIDTH)


def hybrid_layer(x, g_pre_mix, w_in, b_gate, a_re, a_im, log_dt, b_re, b_im, c_re, c_im,
                 d_skip, w_glu, conv_w, b_f, w_branch, w_out, g_post_mix, g_pre_mlp,
                 w_ff1, w_ff2, g_post_mlp):
    bsz, L, _ = x.shape
    h = rmsnorm(x, g_pre_mix)
    proj = jnp.einsum('bld,dn->bln', h, w_in)
    widths = [S5_WIDTH, CONV_WIDTH, CONV_WIDTH, CONV_WIDTH, ATTN_WIDTH, ATTN_WIDTH,
              ATTN_WIDTH, ATTN_HEADS, N_BRANCH * D_MODEL]
    offsets = np.cumsum(widths)[:-1].tolist()
    (u_s5, x_conv, b_conv, c_conv, q, k, v, f_logit, gate_logits) = jnp.split(proj, offsets, axis=-1)

    y_a = s5_mixer(u_s5, a_re, a_im, log_dt, b_re, b_im, c_re, c_im, d_skip, w_glu)
    y_b = short_conv_mixer(x_conv, b_conv, c_conv, conv_w)
    y_c = forgetting_attention(q, k, v, f_logit, b_f)

    gates = jax.nn.sigmoid((gate_logits + b_gate).astype(jnp.float32)).astype(x.dtype)
    gates = gates.reshape(bsz, L, N_BRANCH, D_MODEL)
    wa = w_branch[:S5_WIDTH]
    wb = w_branch[S5_WIDTH:S5_WIDTH + CONV_WIDTH]
    wc = w_branch[S5_WIDTH + CONV_WIDTH:]
    merged = (gates[:, :, 0] * jnp.einsum('blc,cd->bld', y_a, wa)
              + gates[:, :, 1] * jnp.einsum('blc,cd->bld', y_b, wb)
              + gates[:, :, 2] * jnp.einsum('blc,cd->bld', y_c, wc))
    mix = jnp.einsum('bld,de->ble', merged, w_out)
    x = x + rmsnorm(mix, g_post_mix)

    h = rmsnorm(x, g_pre_mlp)
    ff = jnp.einsum('blf,fd->bld', jnp.square(jax.nn.relu(jnp.einsum('bld,df->blf', h, w_ff1))), w_ff2)
    return x + rmsnorm(ff, g_post_mlp)


def setup_inputs(seed: int = 0) -> dict:
    key = jax.random.key(seed)
    ks = jax.random.split(key, 24)
    f32 = jnp.float32
    nrm = lambda k, shape, s: (jax.random.normal(k, shape, f32) * s)
    G, P, H = S5_GROUPS, S5_STATE, S5_GROUP
    x = jax.random.normal(ks[0], (BATCH, SEQ, D_MODEL), f32)
    g_pre_mix = 1.0 + nrm(ks[1], (DEPTH, D_MODEL), 0.05)
    w_in = nrm(ks[2], (DEPTH, D_MODEL, IN_WIDTH), D_MODEL ** -0.5)
    b_gate = nrm(ks[3], (DEPTH, N_BRANCH * D_MODEL), 0.01)
    n_idx = jnp.arange(P, dtype=f32)
    s5_a_re = -0.5 + nrm(ks[4], (DEPTH, G, P), 0.01)
    s5_a_im = math.pi * n_idx + nrm(ks[5], (DEPTH, G, P), 0.01)
    s5_log_dt = jax.random.uniform(ks[6], (DEPTH, G), f32, math.log(1e-3), math.log(1e-1))
    s5_b_re = nrm(ks[7], (DEPTH, G, P, H), (2.0 * H) ** -0.5)
    s5_b_im = nrm(ks[8], (DEPTH, G, P, H), (2.0 * H) ** -0.5)
    s5_c_re = nrm(ks[9], (DEPTH, G, H, P), (2.0 * P) ** -0.5)
    s5_c_im = nrm(ks[10], (DEPTH, G, H, P), (2.0 * P) ** -0.5)
    s5_d = nrm(ks[11], (DEPTH, G, H), 1.0)
    s5_w_glu = nrm(ks[12], (DEPTH, G, H, H), H ** -0.5)
    conv_w = nrm(ks[13], (DEPTH, CONV_K, CONV_WIDTH), CONV_K ** -0.5)
    fox_b_f = jnp.linspace(1.0, 6.0, ATTN_HEADS, dtype=f32)[None, :] + nrm(ks[14], (DEPTH, ATTN_HEADS), 0.1)
    w_branch = nrm(ks[15], (DEPTH, MIX_WIDTH, D_MODEL), S5_WIDTH ** -0.5)
    w_out = nrm(ks[16], (DEPTH, D_MODEL, D_MODEL), D_MODEL ** -0.5)
    g_post_mix = 1.0 + nrm(ks[17], (DEPTH, D_MODEL), 0.05)
    g_pre_mlp = 1.0 + nrm(ks[18], (DEPTH, D_MODEL), 0.05)
    w_ff1 = nrm(ks[19], (DEPTH, D_MODEL, D_FF), D_MODEL ** -0.5)
    w_ff2 = nrm(ks[20], (DEPTH, D_FF, D_MODEL), D_FF ** -0.5)
    g_post_mlp = 1.0 + nrm(ks[21], (DEPTH, D_MODEL), 0.05)
    return {"x": x, "g_pre_mix": g_pre_mix, "w_in": w_in, "b_gate": b_gate,
            "s5_a_re": s5_a_re, "s5_a_im": s5_a_im, "s5_log_dt": s5_log_dt,
            "s5_b_re": s5_b_re, "s5_b_im": s5_b_im, "s5_c_re": s5_c_re, "s5_c_im": s5_c_im,
            "s5_d": s5_d, "s5_w_glu": s5_w_glu, "conv_w": conv_w, "fox_b_f": fox_b_f,
            "w_branch": w_branch, "w_out": w_out, "g_post_mix": g_post_mix,
            "g_pre_mlp": g_pre_mlp, "w_ff1": w_ff1, "w_ff2": w_ff2, "g_post_mlp": g_post_mlp}


def reference(x, g_pre_mix, w_in, b_gate, s5_a_re, s5_a_im, s5_log_dt, s5_b_re, s5_b_im,
              s5_c_re, s5_c_im, s5_d, s5_w_glu, conv_w, fox_b_f, w_branch, w_out,
              g_post_mix, g_pre_mlp, w_ff1, w_ff2, g_post_mlp):
    for layer in range(DEPTH):
        x = hybrid_layer(x, g_pre_mix[layer], w_in[layer], b_gate[layer],
                         s5_a_re[layer], s5_a_im[layer], s5_log_dt[layer],
                         s5_b_re[layer], s5_b_im[layer], s5_c_re[layer], s5_c_im[layer],
                         s5_d[layer], s5_w_glu[layer], conv_w[layer], fox_b_f[layer],
                         w_branch[layer], w_out[layer], g_post_mix[layer], g_pre_mlp[layer],
                         w_ff1[layer], w_ff2[layer], g_post_mlp[layer])
    return x
```

```python
import functools
import math

import jax
import jax.numpy as jnp
from jax import lax
from jax.experimental import pallas as pl
from jax.experimental.pallas import tpu as pltpu

F32 = jnp.float32
BF16 = jnp.bfloat16

EPS = 1e-6
MASK_VALUE = -1e30

S5_GROUP = 16
S5_STATE = 64
ATTN_HEADS = 8
HEAD_DIM = 64
CONV_K = 3
BRANCH_WIDTH = 512

LANES = 128
SUBLANES = 8
MXU_DIM = 256
VMEM_LIMIT = 56 * 1024 * 1024

TM_PROJ = 512
TT_S5 = 64
TQ_ATTN = 256
TK_ATTN = 256


def _rms(x, g):
    return x * lax.rsqrt(jnp.mean(x * x, axis=-1, keepdims=True) + EPS) * g


def _cumsum_lanes(x):
    n = x.shape[-1]
    lane = lax.broadcasted_iota(jnp.int32, x.shape, x.ndim - 1)
    shift = 1
    while shift < n:
        x = x + jnp.where(lane >= shift, pltpu.roll(x, shift, x.ndim - 1), 0.0)
        shift *= 2
    return x


def _const_spec(shape):
    nd = len(shape)
    return pl.BlockSpec(shape, lambda *_: (0,) * nd, pipeline_mode=pl.Buffered(1))


def _inproj_kernel(x_ref, g_ref, w_ref, wft_ref, bf_ref, cw_ref,
                   u_ref, yb_ref, q_ref, k_ref, v_ref, cum_ref,
                   vtail_ref, carry_ref):
    j = pl.program_id(1)
    tm = x_ref.shape[0]
    w = BRANCH_WIDTH

    @pl.when(j == 0)
    def _():
        vtail_ref[...] = jnp.zeros_like(vtail_ref)
        carry_ref[...] = jnp.zeros_like(carry_ref)

    hn = _rms(x_ref[...], g_ref[...]).astype(BF16)

    def proj(c):
        return jnp.dot(hn, w_ref[:, c * w:(c + 1) * w], preferred_element_type=F32)

    u_ref[...] = proj(0).astype(BF16)

    vv = proj(3) * proj(1)
    tail = vtail_ref[...]
    row = lax.broadcasted_iota(jnp.int32, vv.shape, 0)
    v1 = jnp.where(row == 0, tail[7:8, :], pltpu.roll(vv, 1, 0))
    v2 = jnp.where(row == 0, tail[6:7, :],
                   jnp.where(row == 1, tail[7:8, :], pltpu.roll(vv, 2, 0)))
    cw = cw_ref[...]
    conv = v2 * cw[0:1, :] + v1 * cw[1:2, :] + vv * cw[2:3, :]
    yb_ref[...] = (proj(2) * conv).astype(BF16)
    vtail_ref[...] = vv[tm - SUBLANES:tm, :]

    q_ref[...] = (proj(4) * (HEAD_DIM ** -0.5)).astype(BF16)
    k_ref[...] = proj(5).astype(BF16)
    v_ref[...] = proj(6).astype(BF16)

    fl = lax.dot_general(wft_ref[...], hn, (((1,), (1,)), ((), ())),
                         preferred_element_type=F32)
    z = fl + bf_ref[...]
    log_f = jnp.minimum(z, 0.0) - jnp.log1p(jnp.exp(-jnp.abs(z)))
    cum = _cumsum_lanes(log_f) + carry_ref[:, 0:1]
    cum_ref[...] = cum
    carry_ref[...] = jnp.broadcast_to(cum[:, tm - 1:tm], carry_ref.shape)


def _in_proj(x, g, w_a, wf_t, b_f, conv_w):
    bsz, seq, d = x.shape
    tm = TM_PROJ
    nw = w_a.shape[1]
    act = jax.ShapeDtypeStruct((bsz, seq, BRANCH_WIDTH), BF16)
    act_spec = pl.BlockSpec((None, tm, BRANCH_WIDTH), lambda b, j: (b, j, 0))
    return pl.pallas_call(
        _inproj_kernel,
        name="in_proj",
        grid=(bsz, seq // tm),
        in_specs=[
            pl.BlockSpec((None, tm, d), lambda b, j: (b, j, 0)),
            _const_spec((1, d)),
            _const_spec((d, nw)),
            _const_spec((ATTN_HEADS, d)),
            _const_spec((ATTN_HEADS, 1)),
            _const_spec((CONV_K, BRANCH_WIDTH)),
        ],
        out_specs=[act_spec] * 5 + [
            pl.BlockSpec((None, ATTN_HEADS, tm), lambda b, j: (b, 0, j))],
        out_shape=[act] * 5 + [jax.ShapeDtypeStruct((bsz, ATTN_HEADS, seq), F32)],
        scratch_shapes=[pltpu.VMEM((SUBLANES, BRANCH_WIDTH), F32),
                        pltpu.VMEM((ATTN_HEADS, LANES), F32)],
        compiler_params=pltpu.CompilerParams(
            dimension_semantics=("arbitrary", "arbitrary"),
            vmem_limit_bytes=VMEM_LIMIT),
    )(x, g, w_a, wf_t, b_f, conv_w)


def _gelu_tanh(x):
    c = math.sqrt(2.0 / math.pi)
    return 0.5 * x * (1.0 + jnp.tanh(c * (x + 0.044715 * (x * x * x))))


def _s5_kernel(u_ref, bre_ref, bim_ref, cre_ref, cim_ref, are_ref, aim_ref, d_ref, wglu_ref,
               y_ref, utb_ref, sre_ref, sim_ref, stre_ref, stim_ref, ytb_ref):
    i = pl.program_id(0)
    bsz, tt, _ = u_ref.shape
    half_in = MXU_DIM
    half_st = sre_ref.shape[1] // 2

    @pl.when(i == 0)
    def _():
        stre_ref[...] = jnp.zeros_like(stre_ref)
        stim_ref[...] = jnp.zeros_like(stim_ref)

    n_ct = utb_ref.shape[0]
    for b in range(bsz):
        ub_f32 = u_ref[b].astype(F32)
        for c in range(n_ct):
            utb_ref[c, pl.ds(b, tt, stride=bsz), :] = ub_f32[:, c * LANES:(c + 1) * LANES]
    utb = jnp.concatenate([utb_ref[c] for c in range(n_ct)], axis=1)
    ub = utb.astype(BF16)

    for kt in range(2):
        lhs = ub[:, kt * half_in:(kt + 1) * half_in]
        cols = slice(kt * half_st, (kt + 1) * half_st)
        sre_ref[:, cols] = jnp.dot(lhs, bre_ref[kt], preferred_element_type=F32)
        sim_ref[:, cols] = jnp.dot(lhs, bim_ref[kt], preferred_element_type=F32)

    def step(t, carry):
        sr, si = carry
        rows = pl.ds(pl.multiple_of(t * bsz, SUBLANES), bsz)
        ar = are_ref[...]
        ai = aim_ref[...]
        nr = ar * sr - ai * si + sre_ref[rows, :]
        ni = ar * si + ai * sr + sim_ref[rows, :]
        sre_ref[rows, :] = nr
        sim_ref[rows, :] = ni
        return nr, ni

    sr, si = lax.fori_loop(0, tt, step, (stre_ref[...], stim_ref[...]))
    stre_ref[...] = sr
    stim_ref[...] = si

    d = d_ref[...]
    for nt in range(2):
        cols = slice(nt * half_st, (nt + 1) * half_st)
        y = jnp.dot(sre_ref[:, cols].astype(BF16), cre_ref[nt], preferred_element_type=F32)
        y = y + jnp.dot(sim_ref[:, cols].astype(BF16), cim_ref[nt], preferred_element_type=F32)
        ch = slice(nt * half_in, (nt + 1) * half_in)
        y = _gelu_tanh(y + d[:, ch] * utb[:, ch])
        glu = jnp.dot(y.astype(BF16), wglu_ref[nt], preferred_element_type=F32)
        y = y * jax.nn.sigmoid(glu)
        for c in range(half_in // LANES):
            ytb_ref[nt * (half_in // LANES) + c] = y[:, c * LANES:(c + 1) * LANES]

    for b in range(bsz):
        y_ref[b] = jnp.concatenate(
            [ytb_ref[c, pl.ds(b, tt, stride=bsz), :] for c in range(n_ct)], axis=1).astype(BF16)


def _s5_params(a_re, a_im, log_dt, b_re, b_im, c_re, c_im, d_skip, w_glu):
    g, p, h = b_re.shape
    lam_re = jnp.minimum(a_re.astype(F32), -1e-4)
    lam_im = a_im.astype(F32)
    dt = jnp.exp(log_dt.astype(F32))[:, None]
    mag = jnp.exp(lam_re * dt)
    abar_re = mag * jnp.cos(lam_im * dt)
    abar_im = mag * jnp.sin(lam_im * dt)
    inv = 1.0 / (lam_re * lam_re + lam_im * lam_im)
    coef_re = (((abar_re - 1.0) * lam_re + abar_im * lam_im) * inv)[..., None]
    coef_im = ((abar_im * lam_re - (abar_re - 1.0) * lam_im) * inv)[..., None]
    bbar_re = coef_re * b_re.astype(F32) - coef_im * b_im.astype(F32)
    bbar_im = coef_re * b_im.astype(F32) + coef_im * b_re.astype(F32)
    eye = jnp.eye(g, dtype=F32)

    def diag_blocks(full, n_blocks):
        r, c = full.shape[0] // n_blocks, full.shape[1] // n_blocks
        return jnp.stack([full[i * r:(i + 1) * r, i * c:(i + 1) * c] for i in range(n_blocks)])

    def in_mat(m):
        return jnp.einsum('gph,gk->ghkp', m, eye).reshape(g * h, g * p)

    def out_mat(m):
        return jnp.einsum('ghp,gk->gpkh', m, eye).reshape(g * p, g * h)

    bre = diag_blocks(in_mat(bbar_re), 2).astype(BF16)
    bim = diag_blocks(in_mat(bbar_im), 2).astype(BF16)
    cre = diag_blocks(out_mat(c_re.astype(F32)), 2).astype(BF16)
    cim = diag_blocks(out_mat(-c_im.astype(F32)), 2).astype(BF16)
    wglu = jnp.einsum('ghk,gj->ghjk', w_glu.astype(F32), eye).reshape(g * h, g * h)
    wglu = diag_blocks(wglu, 2).astype(BF16)
    are = jnp.broadcast_to(abar_re.reshape(1, g * p), (SUBLANES, g * p))
    aim = jnp.broadcast_to(abar_im.reshape(1, g * p), (SUBLANES, g * p))
    return bre, bim, cre, cim, are, aim, d_skip.astype(F32).reshape(1, g * h), wglu


def _s5(u, params):
    bsz, seq, w = u.shape
    bre, bim, cre, cim, are, aim, d, wglu = params
    n_state = are.shape[1]
    tt = TT_S5
    blk = pl.BlockSpec((bsz, tt, w), lambda i: (0, i, 0))
    consts = (bre, bim, cre, cim, are, aim, d, wglu)
    return pl.pallas_call(
        _s5_kernel,
        name="s5",
        grid=(seq // tt,),
        in_specs=[blk] + [_const_spec(c.shape) for c in consts],
        out_specs=blk,
        out_shape=jax.ShapeDtypeStruct(u.shape, BF16),
        scratch_shapes=[
            pltpu.VMEM((w // LANES, tt * bsz, LANES), F32),
            pltpu.VMEM((tt * bsz, n_state), F32),
            pltpu.VMEM((tt * bsz, n_state), F32),
            pltpu.VMEM((bsz, n_state), F32),
            pltpu.VMEM((bsz, n_state), F32),
            pltpu.VMEM((w // LANES, tt * bsz, LANES), F32),
        ],
        compiler_params=pltpu.CompilerParams(
            dimension_semantics=("arbitrary",),
            vmem_limit_bytes=VMEM_LIMIT),
    )(u, *consts)


def _attn_kernel(q_ref, k_ref, v_ref, cum_ref, o_ref):
    qi = pl.program_id(2)
    tq = q_ref.shape[0]
    tk = TK_ATTN
    q = q_ref[...]
    lane = lax.broadcasted_iota(jnp.int32, q.shape, 1)
    row = lax.broadcasted_iota(jnp.int32, (tq, tk), 0)
    col = lax.broadcasted_iota(jnp.int32, (tq, tk), 1)

    outs = []
    for hh in range(2):
        head_lanes = (lane >= HEAD_DIM) if hh else (lane < HEAD_DIM)
        qh = jnp.where(head_lanes, q, jnp.zeros_like(q))

        def scores(k0):
            s = lax.dot_general(qh, k_ref[pl.ds(k0, tk), :], (((1,), (1,)), ((), ())),
                                preferred_element_type=F32)
            return s - cum_ref[hh:hh + 1, pl.ds(k0, tk)]

        def update(carry, s, k0):
            m, l, acc = carry
            m_new = jnp.maximum(m, jnp.max(s, axis=-1, keepdims=True))
            alpha = jnp.exp(m - m_new)
            p = jnp.exp(s - m_new)
            l = alpha * l + jnp.sum(p, axis=-1, keepdims=True)
            acc = alpha * acc + jnp.dot(p.astype(BF16), v_ref[pl.ds(k0, tk), :],
                                        preferred_element_type=F32)
            return m_new, l, acc

        def full_block(j, carry):
            k0 = pl.multiple_of(j * tk, tk)
            return update(carry, scores(k0), k0)

        init = (jnp.full((tq, 1), MASK_VALUE, F32), jnp.zeros((tq, 1), F32),
                jnp.zeros((tq, LANES), F32))
        carry = lax.fori_loop(0, qi * (tq // tk), full_block, init)
        for jd in range(tq // tk):
            k0 = pl.multiple_of(qi * tq + jd * tk, tk)
            s = jnp.where(col + jd * tk <= row, scores(k0), MASK_VALUE)
            carry = update(carry, s, k0)
        _, l, acc = carry
        outs.append(acc / l)

    o_ref[...] = jnp.where(lane < HEAD_DIM, outs[0], outs[1]).astype(BF16)


def _attention(q, k, v, cum):
    bsz, seq, w = q.shape
    pairs = w // LANES
    cum4 = cum.reshape(bsz, pairs, 2, seq)
    tq = TQ_ATTN
    return pl.pallas_call(
        _attn_kernel,
        name="fox_attn",
        grid=(bsz, pairs, seq // tq),
        in_specs=[
            pl.BlockSpec((None, tq, LANES), lambda b, p, i: (b, i, p)),
            pl.BlockSpec((None, seq, LANES), lambda b, p, i: (b, 0, p)),
            pl.BlockSpec((None, seq, LANES), lambda b, p, i: (b, 0, p)),
            pl.BlockSpec((None, None, 2, seq), lambda b, p, i: (b, p, 0, 0)),
        ],
        out_specs=pl.BlockSpec((None, tq, LANES), lambda b, p, i: (b, i, p)),
        out_shape=jax.ShapeDtypeStruct(q.shape, BF16),
        compiler_params=pltpu.CompilerParams(
            dimension_semantics=("arbitrary", "arbitrary", "arbitrary"),
            vmem_limit_bytes=VMEM_LIMIT),
    )(q, k, v, cum4)


def _merge_kernel(x_ref, ya_ref, yb_ref, yc_ref, gpre_ref, wg_ref, bg_ref, wbr_ref, wout_ref,
                  gpost_ref, o_ref):
    x = x_ref[...]
    d = x.shape[1]
    hn = _rms(x, gpre_ref[...]).astype(BF16)
    merged = None
    for i, y_ref in enumerate((ya_ref, yb_ref, yc_ref)):
        cols = slice(i * d, (i + 1) * d)
        gate = jax.nn.sigmoid(
            jnp.dot(hn, wg_ref[:, cols], preferred_element_type=F32) + bg_ref[:, cols])
        br = jnp.dot(y_ref[...], wbr_ref[i * BRANCH_WIDTH:(i + 1) * BRANCH_WIDTH, :],
                     preferred_element_type=F32)
        merged = gate * br if merged is None else merged + gate * br
    mix = jnp.dot(merged.astype(BF16), wout_ref[...], preferred_element_type=F32)
    o_ref[...] = x + _rms(mix, gpost_ref[...])


def _merge(x, ya, yb, yc, g_pre, w_g, b_g, w_br, w_out, g_post):
    n, d = x.shape
    tm = TM_PROJ
    row_blk = lambda width: pl.BlockSpec((tm, width), lambda i: (i, 0))
    consts = (g_pre, w_g, b_g, w_br, w_out, g_post)
    return pl.pallas_call(
        _merge_kernel,
        name="merge",
        grid=(n // tm,),
        in_specs=[row_blk(d)] + [row_blk(BRANCH_WIDTH)] * 3 + [_const_spec(c.shape) for c in consts],
        out_specs=row_blk(d),
        out_shape=jax.ShapeDtypeStruct(x.shape, F32),
        compiler_params=pltpu.CompilerParams(
            dimension_semantics=("arbitrary",),
            vmem_limit_bytes=VMEM_LIMIT),
    )(x, ya, yb, yc, *consts)


def _mlp_kernel(x_ref, gpre_ref, w1_ref, w2_ref, gpost_ref, o_ref):
    x = x_ref[...]
    d = x.shape[1]
    hn = _rms(x, gpre_ref[...]).astype(BF16)
    acc = None
    for c in range(w1_ref.shape[1] // d):
        cols = slice(c * d, (c + 1) * d)
        h1 = jnp.dot(hn, w1_ref[:, cols], preferred_element_type=F32)
        h1 = jnp.square(jnp.maximum(h1, 0.0)).astype(BF16)
        part = jnp.dot(h1, w2_ref[cols, :], preferred_element_type=F32)
        acc = part if acc is None else acc + part
    o_ref[...] = x + _rms(acc, gpost_ref[...])


def _mlp(x, g_pre, w1, w2, g_post):
    n, d = x.shape
    tm = TM_PROJ
    blk = pl.BlockSpec((tm, d), lambda i: (i, 0))
    consts = (g_pre, w1, w2, g_post)
    return pl.pallas_call(
        _mlp_kernel,
        name="mlp",
        grid=(n // tm,),
        in_specs=[blk] + [_const_spec(c.shape) for c in consts],
        out_specs=blk,
        out_shape=jax.ShapeDtypeStruct(x.shape, F32),
        compiler_params=pltpu.CompilerParams(
            dimension_semantics=("arbitrary",),
            vmem_limit_bytes=VMEM_LIMIT),
    )(x, *consts)


def _layer(x, g_pre_mix, w_in, b_gate, a_re, a_im, log_dt, b_re, b_im, c_re, c_im, d_skip, w_glu,
           conv_w, b_f, w_branch, w_out, g_post_mix, g_pre_mlp, w_ff1, w_ff2, g_post_mlp):
    bsz, seq, d = x.shape
    n_act = 7 * BRANCH_WIDTH
    w_a = w_in[:, :n_act].astype(BF16)
    wf_t = w_in[:, n_act:n_act + ATTN_HEADS].T.astype(BF16)
    w_g = w_in[:, n_act + ATTN_HEADS:].astype(BF16)
    row = lambda v: v.astype(F32).reshape(1, -1)

    u, yb, q, k, v, cum = _in_proj(x, row(g_pre_mix), w_a, wf_t,
                                   b_f.astype(F32).reshape(ATTN_HEADS, 1), conv_w.astype(F32))
    ya = _s5(u, _s5_params(a_re, a_im, log_dt, b_re, b_im, c_re, c_im, d_skip, w_glu))
    yc = _attention(q, k, v, cum)

    n = bsz * seq
    flat = lambda t: t.reshape(n, t.shape[-1])
    x1 = _merge(flat(x), flat(ya), flat(yb), flat(yc), row(g_pre_mix), w_g, row(b_gate),
                w_branch.astype(BF16), w_out.astype(BF16), row(g_post_mix))
    x2 = _mlp(x1, row(g_pre_mlp), w_ff1.astype(BF16), w_ff2.astype(BF16), row(g_post_mlp))
    return x2.reshape(bsz, seq, d)


def kernel(x, g_pre_mix, w_in, b_gate, s5_a_re, s5_a_im, s5_log_dt, s5_b_re, s5_b_im, s5_c_re,
           s5_c_im, s5_d, s5_w_glu, conv_w, fox_b_f, w_branch, w_out, g_post_mix, g_pre_mlp,
           w_ff1, w_ff2, g_post_mlp):
    params = (g_pre_mix, w_in, b_gate, s5_a_re, s5_a_im, s5_log_dt, s5_b_re, s5_b_im, s5_c_re,
              s5_c_im, s5_d, s5_w_glu, conv_w, fox_b_f, w_branch, w_out, g_post_mix, g_pre_mlp,
              w_ff1, w_ff2, g_post_mlp)
    for layer in range(g_pre_mix.shape[0]):
        x = _layer(x, *(p[layer] for p in params))
    return x
```

```python
import functools
import math

import jax
import jax.numpy as jnp
from jax import lax
from jax.experimental import pallas as pl
from jax.experimental.pallas import tpu as pltpu

F32 = jnp.float32
BF16 = jnp.bfloat16

EPS = 1e-6
MASK_VALUE = -1e30

S5_GROUP = 16
S5_STATE = 64
ATTN_HEADS = 8
HEAD_DIM = 64
CONV_K = 3
BRANCH_WIDTH = 512

LANES = 128
SUBLANES = 8
MXU_DIM = 256
VMEM_LIMIT = 56 * 1024 * 1024

TM_PROJ = 512
TT_S5 = 64
T_ATTN = 512
LOG2E = math.log2(math.e)


def _rms(x, g):
    return x * lax.rsqrt(jnp.mean(x * x, axis=-1, keepdims=True) + EPS) * g


def _cumsum_lanes(x):
    n = x.shape[-1]
    lane = lax.broadcasted_iota(jnp.int32, x.shape, x.ndim - 1)
    shift = 1
    while shift < n:
        x = x + jnp.where(lane >= shift, pltpu.roll(x, shift, x.ndim - 1), 0.0)
        shift *= 2
    return x


def _const_spec(shape):
    nd = len(shape)
    return pl.BlockSpec(shape, lambda *_: (0,) * nd, pipeline_mode=pl.Buffered(1))


def _inproj_kernel(x_ref, g_ref, w_ref, wft_ref, bf_ref, cw_ref,
                   u_ref, yb_ref, q_ref, k_ref, v_ref, cum_ref,
                   vtail_ref, carry_ref):
    j = pl.program_id(1)
    tm = x_ref.shape[0]
    w = BRANCH_WIDTH

    @pl.when(j == 0)
    def _():
        vtail_ref[...] = jnp.zeros_like(vtail_ref)
        carry_ref[...] = jnp.zeros_like(carry_ref)

    hn = _rms(x_ref[...], g_ref[...]).astype(BF16)

    def proj(c):
        return jnp.dot(hn, w_ref[:, c * w:(c + 1) * w], preferred_element_type=F32)

    u_ref[...] = proj(0).astype(BF16)

    vv = proj(3) * proj(1)
    tail = vtail_ref[...]
    row = lax.broadcasted_iota(jnp.int32, vv.shape, 0)
    v1 = jnp.where(row == 0, tail[7:8, :], pltpu.roll(vv, 1, 0))
    v2 = jnp.where(row == 0, tail[6:7, :],
                   jnp.where(row == 1, tail[7:8, :], pltpu.roll(vv, 2, 0)))
    cw = cw_ref[...]
    conv = v2 * cw[0:1, :] + v1 * cw[1:2, :] + vv * cw[2:3, :]
    yb_ref[...] = (proj(2) * conv).astype(BF16)
    vtail_ref[...] = vv[tm - SUBLANES:tm, :]

    q_ref[...] = (proj(4) * (HEAD_DIM ** -0.5 * LOG2E)).astype(BF16)
    k_ref[...] = proj(5).astype(BF16)
    v_ref[...] = proj(6).astype(BF16)

    fl = lax.dot_general(wft_ref[...], hn, (((1,), (1,)), ((), ())),
                         preferred_element_type=F32)
    z = fl + bf_ref[...]
    log_f = (jnp.minimum(z, 0.0) - jnp.log1p(jnp.exp(-jnp.abs(z)))) * LOG2E
    cum = _cumsum_lanes(log_f) + carry_ref[:, 0:1]
    cum_ref[...] = cum
    carry_ref[...] = jnp.broadcast_to(cum[:, tm - 1:tm], carry_ref.shape)


def _in_proj(x, g, w_a, wf_t, b_f, conv_w):
    bsz, seq, d = x.shape
    tm = TM_PROJ
    nw = w_a.shape[1]
    act = jax.ShapeDtypeStruct((bsz, seq, BRANCH_WIDTH), BF16)
    act_spec = pl.BlockSpec((None, tm, BRANCH_WIDTH), lambda b, j: (b, j, 0))
    return pl.pallas_call(
        _inproj_kernel,
        name="in_proj",
        grid=(bsz, seq // tm),
        in_specs=[
            pl.BlockSpec((None, tm, d), lambda b, j: (b, j, 0)),
            _const_spec((1, d)),
            _const_spec((d, nw)),
            _const_spec((ATTN_HEADS, d)),
            _const_spec((ATTN_HEADS, 1)),
            _const_spec((CONV_K, BRANCH_WIDTH)),
        ],
        out_specs=[act_spec] * 5 + [
            pl.BlockSpec((None, ATTN_HEADS, tm), lambda b, j: (b, 0, j))],
        out_shape=[act] * 5 + [jax.ShapeDtypeStruct((bsz, ATTN_HEADS, seq), F32)],
        scratch_shapes=[pltpu.VMEM((SUBLANES, BRANCH_WIDTH), F32),
                        pltpu.VMEM((ATTN_HEADS, LANES), F32)],
        compiler_params=pltpu.CompilerParams(
            dimension_semantics=("arbitrary", "arbitrary"),
            vmem_limit_bytes=VMEM_LIMIT),
    )(x, g, w_a, wf_t, b_f, conv_w)


def _gelu_tanh(x):
    c = math.sqrt(2.0 / math.pi)
    return 0.5 * x * (1.0 + jnp.tanh(c * (x + 0.044715 * (x * x * x))))


def _s5_kernel(u_ref, bre_ref, bim_ref, cre_ref, cim_ref, are_ref, aim_ref, d_ref, wglu_ref,
               y_ref, utb_ref, sre_ref, sim_ref, stre_ref, stim_ref, ytb_ref):
    i = pl.program_id(0)
    bsz, tt, _ = u_ref.shape
    half_in = MXU_DIM
    half_st = sre_ref.shape[1] // 2

    @pl.when(i == 0)
    def _():
        stre_ref[...] = jnp.zeros_like(stre_ref)
        stim_ref[...] = jnp.zeros_like(stim_ref)

    n_ct = utb_ref.shape[0]
    for b in range(bsz):
        ub_f32 = u_ref[b].astype(F32)
        for c in range(n_ct):
            utb_ref[c, pl.ds(b, tt, stride=bsz), :] = ub_f32[:, c * LANES:(c + 1) * LANES]
    utb = jnp.concatenate([utb_ref[c] for c in range(n_ct)], axis=1)
    ub = utb.astype(BF16)

    for kt in range(2):
        lhs = ub[:, kt * half_in:(kt + 1) * half_in]
        cols = slice(kt * half_st, (kt + 1) * half_st)
        sre_ref[:, cols] = jnp.dot(lhs, bre_ref[kt], preferred_element_type=F32)
        sim_ref[:, cols] = jnp.dot(lhs, bim_ref[kt], preferred_element_type=F32)

    def step(t, carry):
        sr, si = carry
        rows = pl.ds(pl.multiple_of(t * bsz, SUBLANES), bsz)
        ar = are_ref[...]
        ai = aim_ref[...]
        nr = ar * sr - ai * si + sre_ref[rows, :]
        ni = ar * si + ai * sr + sim_ref[rows, :]
        sre_ref[rows, :] = nr
        sim_ref[rows, :] = ni
        return nr, ni

    sr, si = lax.fori_loop(0, tt, step, (stre_ref[...], stim_ref[...]))
    stre_ref[...] = sr
    stim_ref[...] = si

    d = d_ref[...]
    for nt in range(2):
        cols = slice(nt * half_st, (nt + 1) * half_st)
        y = jnp.dot(sre_ref[:, cols].astype(BF16), cre_ref[nt], preferred_element_type=F32)
        y = y + jnp.dot(sim_ref[:, cols].astype(BF16), cim_ref[nt], preferred_element_type=F32)
        ch = slice(nt * half_in, (nt + 1) * half_in)
        y = _gelu_tanh(y + d[:, ch] * utb[:, ch])
        glu = jnp.dot(y.astype(BF16), wglu_ref[nt], preferred_element_type=F32)
        y = y * jax.nn.sigmoid(glu)
        for c in range(half_in // LANES):
            ytb_ref[nt * (half_in // LANES) + c] = y[:, c * LANES:(c + 1) * LANES]

    for b in range(bsz):
        y_ref[b] = jnp.concatenate(
            [ytb_ref[c, pl.ds(b, tt, stride=bsz), :] for c in range(n_ct)], axis=1).astype(BF16)


def _s5_params(a_re, a_im, log_dt, b_re, b_im, c_re, c_im, d_skip, w_glu):
    g, p, h = b_re.shape
    lam_re = jnp.minimum(a_re.astype(F32), -1e-4)
    lam_im = a_im.astype(F32)
    dt = jnp.exp(log_dt.astype(F32))[:, None]
    mag = jnp.exp(lam_re * dt)
    abar_re = mag * jnp.cos(lam_im * dt)
    abar_im = mag * jnp.sin(lam_im * dt)
    inv = 1.0 / (lam_re * lam_re + lam_im * lam_im)
    coef_re = (((abar_re - 1.0) * lam_re + abar_im * lam_im) * inv)[..., None]
    coef_im = ((abar_im * lam_re - (abar_re - 1.0) * lam_im) * inv)[..., None]
    bbar_re = coef_re * b_re.astype(F32) - coef_im * b_im.astype(F32)
    bbar_im = coef_re * b_im.astype(F32) + coef_im * b_re.astype(F32)
    eye = jnp.eye(g, dtype=F32)

    def diag_blocks(full, n_blocks):
        r, c = full.shape[0] // n_blocks, full.shape[1] // n_blocks
        return jnp.stack([full[i * r:(i + 1) * r, i * c:(i + 1) * c] for i in range(n_blocks)])

    def in_mat(m):
        return jnp.einsum('gph,gk->ghkp', m, eye).reshape(g * h, g * p)

    def out_mat(m):
        return jnp.einsum('ghp,gk->gpkh', m, eye).reshape(g * p, g * h)

    bre = diag_blocks(in_mat(bbar_re), 2).astype(BF16)
    bim = diag_blocks(in_mat(bbar_im), 2).astype(BF16)
    cre = diag_blocks(out_mat(c_re.astype(F32)), 2).astype(BF16)
    cim = diag_blocks(out_mat(-c_im.astype(F32)), 2).astype(BF16)
    wglu = jnp.einsum('ghk,gj->ghjk', w_glu.astype(F32), eye).reshape(g * h, g * h)
    wglu = diag_blocks(wglu, 2).astype(BF16)
    are = jnp.broadcast_to(abar_re.reshape(1, g * p), (SUBLANES, g * p))
    aim = jnp.broadcast_to(abar_im.reshape(1, g * p), (SUBLANES, g * p))
    return bre, bim, cre, cim, are, aim, d_skip.astype(F32).reshape(1, g * h), wglu


def _s5(u, params):
    bsz, seq, w = u.shape
    bre, bim, cre, cim, are, aim, d, wglu = params
    n_state = are.shape[1]
    tt = TT_S5
    blk = pl.BlockSpec((bsz, tt, w), lambda i: (0, i, 0))
    consts = (bre, bim, cre, cim, are, aim, d, wglu)
    return pl.pallas_call(
        _s5_kernel,
        name="s5",
        grid=(seq // tt,),
        in_specs=[blk] + [_const_spec(c.shape) for c in consts],
        out_specs=blk,
        out_shape=jax.ShapeDtypeStruct(u.shape, BF16),
        scratch_shapes=[
            pltpu.VMEM((w // LANES, tt * bsz, LANES), F32),
            pltpu.VMEM((tt * bsz, n_state), F32),
            pltpu.VMEM((tt * bsz, n_state), F32),
            pltpu.VMEM((bsz, n_state), F32),
            pltpu.VMEM((bsz, n_state), F32),
            pltpu.VMEM((w // LANES, tt * bsz, LANES), F32),
        ],
        compiler_params=pltpu.CompilerParams(
            dimension_semantics=("arbitrary",),
            vmem_limit_bytes=VMEM_LIMIT),
    )(u, *consts)


def _attn_kernel(q_ref, k_ref, v_ref, cum_ref, o_ref):
    qi = pl.program_id(2)
    tq = q_ref.shape[0]
    tk = tq
    q = q_ref[...]
    lane = lax.broadcasted_iota(jnp.int32, q.shape, 1)
    row = lax.broadcasted_iota(jnp.int32, (tq, tk), 0)
    col = lax.broadcasted_iota(jnp.int32, (tq, tk), 1)

    qh = (jnp.where(lane < HEAD_DIM, q, jnp.zeros_like(q)),
          jnp.where(lane >= HEAD_DIM, q, jnp.zeros_like(q)))

    def block(k0, carry, masked):
        kb = k_ref[pl.ds(k0, tk), :]
        vb = v_ref[pl.ds(k0, tk), :]
        out = []
        for hh in range(2):
            m, l, acc = carry[hh]
            s = lax.dot_general(qh[hh], kb, (((1,), (1,)), ((), ())),
                                preferred_element_type=F32)
            s = s - cum_ref[hh:hh + 1, pl.ds(k0, tk)]
            if masked:
                s = jnp.where(col <= row, s, MASK_VALUE)
            m_new = jnp.maximum(m, jnp.max(s, axis=-1, keepdims=True))
            alpha = jnp.exp2(m - m_new)
            p = jnp.exp2(s - m_new)
            psum = p[:, 0:LANES]
            for c in range(1, tk // LANES):
                psum = psum + p[:, c * LANES:(c + 1) * LANES]
            l = alpha * l + psum
            acc = alpha * acc + jnp.dot(p.astype(BF16), vb, preferred_element_type=F32)
            out.append((m_new, l, acc))
        return tuple(out)

    init_h = (jnp.full((tq, 1), MASK_VALUE, F32), jnp.zeros((tq, LANES), F32),
              jnp.zeros((tq, LANES), F32))
    carry = lax.fori_loop(
        0, qi, lambda j, c: block(pl.multiple_of(j * tk, tk), c, False), (init_h, init_h))
    carry = block(pl.multiple_of(qi * tk, tk), carry, True)
    outs = [acc / jnp.sum(l, axis=-1, keepdims=True) for _, l, acc in carry]
    o_ref[...] = jnp.where(lane < HEAD_DIM, outs[0], outs[1]).astype(BF16)


def _attention(q, k, v, cum):
    bsz, seq, w = q.shape
    pairs = w // LANES
    cum4 = cum.reshape(bsz, pairs, 2, seq)
    tq = T_ATTN
    return pl.pallas_call(
        _attn_kernel,
        name="fox_attn",
        grid=(bsz, pairs, seq // tq),
        in_specs=[
            pl.BlockSpec((None, tq, LANES), lambda b, p, i: (b, i, p)),
            pl.BlockSpec((None, seq, LANES), lambda b, p, i: (b, 0, p)),
            pl.BlockSpec((None, seq, LANES), lambda b, p, i: (b, 0, p)),
            pl.BlockSpec((None, None, 2, seq), lambda b, p, i: (b, p, 0, 0)),
        ],
        out_specs=pl.BlockSpec((None, tq, LANES), lambda b, p, i: (b, i, p)),
        out_shape=jax.ShapeDtypeStruct(q.shape, BF16),
        compiler_params=pltpu.CompilerParams(
            dimension_semantics=("arbitrary", "arbitrary", "arbitrary"),
            vmem_limit_bytes=VMEM_LIMIT),
    )(q, k, v, cum4)


def _merge_kernel(x_ref, ya_ref, yb_ref, yc_ref, gpre_ref, wg_ref, bg_ref, wbr_ref, wout_ref,
                  gpost_ref, o_ref):
    x = x_ref[...]
    d = x.shape[1]
    hn = _rms(x, gpre_ref[...]).astype(BF16)
    merged = None
    for i, y_ref in enumerate((ya_ref, yb_ref, yc_ref)):
        cols = slice(i * d, (i + 1) * d)
        gate = jax.nn.sigmoid(
            jnp.dot(hn, wg_ref[:, cols], preferred_element_type=F32) + bg_ref[:, cols])
        br = jnp.dot(y_ref[...], wbr_ref[i * BRANCH_WIDTH:(i + 1) * BRANCH_WIDTH, :],
                     preferred_element_type=F32)
        merged = gate * br if merged is None else merged + gate * br
    mix = jnp.dot(merged.astype(BF16), wout_ref[...], preferred_element_type=F32)
    o_ref[...] = x + _rms(mix, gpost_ref[...])


def _merge(x, ya, yb, yc, g_pre, w_g, b_g, w_br, w_out, g_post):
    n, d = x.shape
    tm = TM_PROJ
    row_blk = lambda width: pl.BlockSpec((tm, width), lambda i: (i, 0))
    consts = (g_pre, w_g, b_g, w_br, w_out, g_post)
    return pl.pallas_call(
        _merge_kernel,
        name="merge",
        grid=(n // tm,),
        in_specs=[row_blk(d)] + [row_blk(BRANCH_WIDTH)] * 3 + [_const_spec(c.shape) for c in consts],
        out_specs=row_blk(d),
        out_shape=jax.ShapeDtypeStruct(x.shape, F32),
        compiler_params=pltpu.CompilerParams(
            dimension_semantics=("arbitrary",),
            vmem_limit_bytes=VMEM_LIMIT),
    )(x, ya, yb, yc, *consts)


def _mlp_kernel(x_ref, gpre_ref, w1_ref, w2_ref, gpost_ref, o_ref):
    x = x_ref[...]
    d = x.shape[1]
    hn = _rms(x, gpre_ref[...]).astype(BF16)
    acc = None
    for c in range(w1_ref.shape[1] // d):
        cols = slice(c * d, (c + 1) * d)
        h1 = jnp.dot(hn, w1_ref[:, cols], preferred_element_type=F32)
        h1 = jnp.square(jnp.maximum(h1, 0.0)).astype(BF16)
        part = jnp.dot(h1, w2_ref[cols, :], preferred_element_type=F32)
        acc = part if acc is None else acc + part
    o_ref[...] = x + _rms(acc, gpost_ref[...])


def _mlp(x, g_pre, w1, w2, g_post):
    n, d = x.shape
    tm = TM_PROJ
    blk = pl.BlockSpec((tm, d), lambda i: (i, 0))
    consts = (g_pre, w1, w2, g_post)
    return pl.pallas_call(
        _mlp_kernel,
        name="mlp",
        grid=(n // tm,),
        in_specs=[blk] + [_const_spec(c.shape) for c in consts],
        out_specs=blk,
        out_shape=jax.ShapeDtypeStruct(x.shape, F32),
        compiler_params=pltpu.CompilerParams(
            dimension_semantics=("arbitrary",),
            vmem_limit_bytes=VMEM_LIMIT),
    )(x, *consts)


def _layer(x, g_pre_mix, w_in, b_gate, a_re, a_im, log_dt, b_re, b_im, c_re, c_im, d_skip, w_glu,
           conv_w, b_f, w_branch, w_out, g_post_mix, g_pre_mlp, w_ff1, w_ff2, g_post_mlp):
    bsz, seq, d = x.shape
    n_act = 7 * BRANCH_WIDTH
    w_a = w_in[:, :n_act].astype(BF16)
    wf_t = w_in[:, n_act:n_act + ATTN_HEADS].T.astype(BF16)
    w_g = w_in[:, n_act + ATTN_HEADS:].astype(BF16)
    row = lambda v: v.astype(F32).reshape(1, -1)

    u, yb, q, k, v, cum = _in_proj(x, row(g_pre_mix), w_a, wf_t,
                                   b_f.astype(F32).reshape(ATTN_HEADS, 1), conv_w.astype(F32))
    ya = _s5(u, _s5_params(a_re, a_im, log_dt, b_re, b_im, c_re, c_im, d_skip, w_glu))
    yc = _attention(q, k, v, cum)

    n = bsz * seq
    flat = lambda t: t.reshape(n, t.shape[-1])
    x1 = _merge(flat(x), flat(ya), flat(yb), flat(yc), row(g_pre_mix), w_g, row(b_gate),
                w_branch.astype(BF16), w_out.astype(BF16), row(g_post_mix))
    x2 = _mlp(x1, row(g_pre_mlp), w_ff1.astype(BF16), w_ff2.astype(BF16), row(g_post_mlp))
    return x2.reshape(bsz, seq, d)


def kernel(x, g_pre_mix, w_in, b_gate, s5_a_re, s5_a_im, s5_log_dt, s5_b_re, s5_b_im, s5_c_re,
           s5_c_im, s5_d, s5_w_glu, conv_w, fox_b_f, w_branch, w_out, g_post_mix, g_pre_mlp,
           w_ff1, w_ff2, g_post_mlp):
    params = (g_pre_mix, w_in, b_gate, s5_a_re, s5_a_im, s5_log_dt, s5_b_re, s5_b_im, s5_c_re,
              s5_c_im, s5_d, s5_w_glu, conv_w, fox_b_f, w_branch, w_out, g_post_mix, g_pre_mlp,
              w_ff1, w_ff2, g_post_mlp)
    for layer in range(g_pre_mix.shape[0]):
        x = _layer(x, *(p[layer] for p in params))
    return x
```

```python
import functools
import math

import jax
import jax.numpy as jnp
from jax import lax
from jax.experimental import pallas as pl
from jax.experimental.pallas import tpu as pltpu

F32 = jnp.float32
BF16 = jnp.bfloat16

EPS = 1e-6
MASK_VALUE = -1e30

S5_GROUP = 16
S5_STATE = 64
ATTN_HEADS = 8
HEAD_DIM = 64
CONV_K = 3
BRANCH_WIDTH = 512

LANES = 128
SUBLANES = 8
BF16_SUBLANES = 16
MXU_DIM = 256
VMEM_LIMIT = 56 * 1024 * 1024

TM_PROJ = 512
TT_S5 = 64
T_ATTN = 512
TS_ATTN = 256
LOG2E = math.log2(math.e)


def _rms(x, g):
    return x * lax.rsqrt(jnp.mean(x * x, axis=-1, keepdims=True) + EPS) * g


def _cumsum_lanes(x):
    n = x.shape[-1]
    lane = lax.broadcasted_iota(jnp.int32, x.shape, x.ndim - 1)
    shift = 1
    while shift < n:
        x = x + jnp.where(lane >= shift, pltpu.roll(x, shift, x.ndim - 1), 0.0)
        shift *= 2
    return x


def _layer_spec(arr, layer, block=None):
    block = arr.shape[1:] if block is None else block
    return pl.BlockSpec((None,) + tuple(block), lambda *_: (layer,) + (0,) * len(block),
                        pipeline_mode=pl.Buffered(1))


def _inproj_kernel(x_ref, g_ref, w_ref, wvt_ref, wft_ref, bf_ref, cw_ref,
                   u_ref, yb_ref, q_ref, k_ref, vt_ref, cum_ref,
                   vtail_ref, carry_ref):
    j = pl.program_id(1)
    tm = x_ref.shape[0]
    w = BRANCH_WIDTH

    @pl.when(j == 0)
    def _():
        vtail_ref[...] = jnp.zeros_like(vtail_ref)
        carry_ref[...] = jnp.zeros_like(carry_ref)

    hn = _rms(x_ref[...], g_ref[...]).astype(BF16)

    def proj(c):
        return jnp.dot(hn, w_ref[:, c * w:(c + 1) * w], preferred_element_type=F32)

    u_ref[...] = proj(0).astype(BF16)

    vv = proj(3) * proj(1)
    tail = vtail_ref[...]
    row = lax.broadcasted_iota(jnp.int32, vv.shape, 0)
    v1 = jnp.where(row == 0, tail[7:8, :], pltpu.roll(vv, 1, 0))
    v2 = jnp.where(row == 0, tail[6:7, :],
                   jnp.where(row == 1, tail[7:8, :], pltpu.roll(vv, 2, 0)))
    cw = cw_ref[...]
    conv = v2 * cw[0:1, :] + v1 * cw[1:2, :] + vv * cw[2:3, :]
    yb_ref[...] = (proj(2) * conv).astype(BF16)
    vtail_ref[...] = vv[tm - SUBLANES:tm, :]

    q_ref[...] = (proj(4) * (HEAD_DIM ** -0.5 * LOG2E)).astype(BF16)
    k_ref[...] = proj(5).astype(BF16)
    nt_dims = (((1,), (1,)), ((), ()))
    vt_ref[...] = lax.dot_general(wvt_ref[...], hn, nt_dims,
                                  preferred_element_type=F32).astype(BF16)

    fl = lax.dot_general(wft_ref[...], hn, nt_dims, preferred_element_type=F32)
    z = fl + bf_ref[...]
    log_f = (jnp.minimum(z, 0.0) - jnp.log1p(jnp.exp(-jnp.abs(z)))) * LOG2E
    cum = _cumsum_lanes(log_f) + carry_ref[:, 0:1]
    carry_ref[...] = jnp.broadcast_to(cum[:, tm - 1:tm], carry_ref.shape)
    cum_t = jnp.concatenate([cum, jnp.zeros((LANES - ATTN_HEADS, tm), F32)], axis=0).T
    for p in range(ATTN_HEADS // 2):
        cum_ref[p] = cum_t[:, 2 * p:2 * p + 2]


def _in_proj(layer, x, g, w_in, wv_t, wf_t, b_f, conv_w):
    bsz, seq, d = x.shape
    tm = TM_PROJ
    pairs = ATTN_HEADS // 2
    n_act = 6 * BRANCH_WIDTH
    act = jax.ShapeDtypeStruct((bsz, seq, BRANCH_WIDTH), BF16)
    act_spec = pl.BlockSpec((None, tm, BRANCH_WIDTH), lambda b, j: (b, j, 0))
    return pl.pallas_call(
        _inproj_kernel,
        name="in_proj",
        grid=(bsz, seq // tm),
        in_specs=[
            pl.BlockSpec((None, tm, d), lambda b, j: (b, j, 0)),
            _layer_spec(g, layer),
            _layer_spec(w_in, layer, (d, n_act)),
            _layer_spec(wv_t, layer),
            _layer_spec(wf_t, layer),
            _layer_spec(b_f, layer),
            _layer_spec(conv_w, layer),
        ],
        out_specs=[act_spec] * 4 + [
            pl.BlockSpec((None, BRANCH_WIDTH, tm), lambda b, j: (b, 0, j)),
            pl.BlockSpec((None, pairs, tm, 2), lambda b, j: (b, 0, j, 0))],
        out_shape=[act] * 4 + [
            jax.ShapeDtypeStruct((bsz, BRANCH_WIDTH, seq), BF16),
            jax.ShapeDtypeStruct((bsz, pairs, seq, 2), F32)],
        scratch_shapes=[pltpu.VMEM((SUBLANES, BRANCH_WIDTH), F32),
                        pltpu.VMEM((ATTN_HEADS, LANES), F32)],
        compiler_params=pltpu.CompilerParams(
            dimension_semantics=("arbitrary", "arbitrary"),
            vmem_limit_bytes=VMEM_LIMIT),
    )(x, g, w_in, wv_t, wf_t, b_f, conv_w)


def _gelu_tanh(x):
    c = math.sqrt(2.0 / math.pi)
    return 0.5 * x * (1.0 + jnp.tanh(c * (x + 0.044715 * (x * x * x))))


def _s5_kernel(u_ref, bre_ref, bim_ref, cre_ref, cim_ref, are_ref, aim_ref, d_ref, wglu_ref,
               y_ref, utb_ref, sre_ref, sim_ref, stre_ref, stim_ref, ytb_ref):
    i = pl.program_id(0)
    bsz, tt, _ = u_ref.shape
    half_in = MXU_DIM
    half_st = sre_ref.shape[1] // 2

    @pl.when(i == 0)
    def _():
        stre_ref[...] = jnp.zeros_like(stre_ref)
        stim_ref[...] = jnp.zeros_like(stim_ref)

    n_ct = utb_ref.shape[0]
    for b in range(bsz):
        ub_f32 = u_ref[b].astype(F32)
        for c in range(n_ct):
            utb_ref[c, pl.ds(b, tt, stride=bsz), :] = ub_f32[:, c * LANES:(c + 1) * LANES]
    utb = jnp.concatenate([utb_ref[c] for c in range(n_ct)], axis=1)
    ub = utb.astype(BF16)

    for kt in range(2):
        lhs = ub[:, kt * half_in:(kt + 1) * half_in]
        cols = slice(kt * half_st, (kt + 1) * half_st)
        sre_ref[:, cols] = jnp.dot(lhs, bre_ref[kt], preferred_element_type=F32)
        sim_ref[:, cols] = jnp.dot(lhs, bim_ref[kt], preferred_element_type=F32)

    def step(t, carry):
        sr, si = carry
        rows = pl.ds(pl.multiple_of(t * bsz, SUBLANES), bsz)
        ar = are_ref[...]
        ai = aim_ref[...]
        nr = ar * sr - ai * si + sre_ref[rows, :]
        ni = ar * si + ai * sr + sim_ref[rows, :]
        sre_ref[rows, :] = nr
        sim_ref[rows, :] = ni
        return nr, ni

    sr, si = lax.fori_loop(0, tt, step, (stre_ref[...], stim_ref[...]))
    stre_ref[...] = sr
    stim_ref[...] = si

    d = d_ref[...]
    for nt in range(2):
        cols = slice(nt * half_st, (nt + 1) * half_st)
        y = jnp.dot(sre_ref[:, cols].astype(BF16), cre_ref[nt], preferred_element_type=F32)
        y = y + jnp.dot(sim_ref[:, cols].astype(BF16), cim_ref[nt], preferred_element_type=F32)
        ch = slice(nt * half_in, (nt + 1) * half_in)
        y = _gelu_tanh(y + d[:, ch] * utb[:, ch])
        glu = jnp.dot(y.astype(BF16), wglu_ref[nt], preferred_element_type=F32)
        y = y * jax.nn.sigmoid(glu)
        for c in range(half_in // LANES):
            ytb_ref[nt * (half_in // LANES) + c] = y[:, c * LANES:(c + 1) * LANES]

    for b in range(bsz):
        y_ref[b] = jnp.concatenate(
            [ytb_ref[c, pl.ds(b, tt, stride=bsz), :] for c in range(n_ct)], axis=1).astype(BF16)


def _s5_params(a_re, a_im, log_dt, b_re, b_im, c_re, c_im, d_skip, w_glu):
    depth, g, p, h = b_re.shape
    lam_re = jnp.minimum(a_re.astype(F32), -1e-4)
    lam_im = a_im.astype(F32)
    dt = jnp.exp(log_dt.astype(F32))[..., None]
    mag = jnp.exp(lam_re * dt)
    abar_re = mag * jnp.cos(lam_im * dt)
    abar_im = mag * jnp.sin(lam_im * dt)
    inv = 1.0 / (lam_re * lam_re + lam_im * lam_im)
    coef_re = (((abar_re - 1.0) * lam_re + abar_im * lam_im) * inv)[..., None]
    coef_im = ((abar_im * lam_re - (abar_re - 1.0) * lam_im) * inv)[..., None]
    bbar_re = coef_re * b_re.astype(F32) - coef_im * b_im.astype(F32)
    bbar_im = coef_re * b_im.astype(F32) + coef_im * b_re.astype(F32)
    gb = g // 2
    eye = jnp.eye(gb, dtype=BF16)

    def blocks(m, spec, rows, cols):
        m = m.astype(BF16).reshape(depth, 2, gb, m.shape[2], m.shape[3])
        return jnp.einsum(spec, m, eye).reshape(depth, 2, rows, cols)

    bre = blocks(bbar_re, 'dkgph,gj->dkghjp', gb * h, gb * p)
    bim = blocks(bbar_im, 'dkgph,gj->dkghjp', gb * h, gb * p)
    cre = blocks(c_re, 'dkghp,gj->dkgpjh', gb * p, gb * h)
    cim = blocks(-c_im.astype(F32), 'dkghp,gj->dkgpjh', gb * p, gb * h)
    wglu = blocks(w_glu, 'dkghc,gj->dkghjc', gb * h, gb * h)
    are = jnp.broadcast_to(abar_re.reshape(depth, 1, g * p), (depth, SUBLANES, g * p))
    aim = jnp.broadcast_to(abar_im.reshape(depth, 1, g * p), (depth, SUBLANES, g * p))
    return bre, bim, cre, cim, are, aim, d_skip.astype(F32).reshape(depth, 1, g * h), wglu


def _s5(layer, u, params):
    bsz, seq, w = u.shape
    consts = params
    n_state = params[4].shape[-1]
    tt = TT_S5
    blk = pl.BlockSpec((bsz, tt, w), lambda i: (0, i, 0))
    return pl.pallas_call(
        _s5_kernel,
        name="s5",
        grid=(seq // tt,),
        in_specs=[blk] + [_layer_spec(c, layer) for c in consts],
        out_specs=blk,
        out_shape=jax.ShapeDtypeStruct(u.shape, BF16),
        scratch_shapes=[
            pltpu.VMEM((w // LANES, tt * bsz, LANES), F32),
            pltpu.VMEM((tt * bsz, n_state), F32),
            pltpu.VMEM((tt * bsz, n_state), F32),
            pltpu.VMEM((bsz, n_state), F32),
            pltpu.VMEM((bsz, n_state), F32),
            pltpu.VMEM((w // LANES, tt * bsz, LANES), F32),
        ],
        compiler_params=pltpu.CompilerParams(
            dimension_semantics=("arbitrary",),
            vmem_limit_bytes=VMEM_LIMIT),
    )(u, *consts)


def _attn_kernel(q_ref, k_ref, vt_ref, cum_ref, o_ref, s_ref, bmax_ref, m_ref, acc_ref):
    qi = pl.program_id(2)
    tq = q_ref.shape[0]
    ts = TS_ATTN
    n_strips = tq // ts
    assert n_strips == 2
    q = q_ref[...]
    lane = lax.broadcasted_iota(jnp.int32, (ts, LANES), 1)
    k_pos = lax.broadcasted_iota(jnp.int32, (ts, ts), 0)
    q_pos = lax.broadcasted_iota(jnp.int32, (ts, ts), 1)
    nt_dims = (((1,), (1,)), ((), ()))

    chains = [(st, hh) for st in range(n_strips) for hh in range(2)]
    qs = {}
    for st, hh in chains:
        q_st = q[st * ts:(st + 1) * ts, :]
        head_lanes = (lane >= HEAD_DIM) if hh else (lane < HEAD_DIM)
        qs[st, hh] = jnp.where(head_lanes, q_st, jnp.zeros_like(q_st))

    def key_start(blk):
        return pl.multiple_of(blk * ts, ts)

    def scores(blk, buf, first_strip=0):
        k0 = key_start(blk)
        kb = k_ref[pl.ds(k0, ts), :]
        for c, (st, hh) in enumerate(chains):
            if st < first_strip:
                continue
            s = lax.dot_general(kb, qs[st, hh], nt_dims, preferred_element_type=F32)
            s = s - cum_ref[pl.ds(k0, ts), hh:hh + 1]
            s_ref[buf, c] = s
            bmax_ref[buf, c] = jnp.max(s, axis=0, keepdims=True)

    def softmax_pv(blk, buf, first_strip=0, diag_strip=None):
        vt = vt_ref[:, pl.ds(key_start(blk), ts)]
        vt_ones = [jnp.concatenate([vt[hh * HEAD_DIM:(hh + 1) * HEAD_DIM, :], ones_rows], axis=0)
                   for hh in range(2)]
        for c, (st, hh) in enumerate(chains):
            if st < first_strip:
                continue
            s = s_ref[buf, c]
            blk_max = bmax_ref[buf, c]
            if st == diag_strip:
                s = jnp.where(k_pos <= q_pos, s, MASK_VALUE)
                blk_max = jnp.max(s, axis=0, keepdims=True)
            m = m_ref[c]
            m_new = jnp.maximum(m, blk_max)
            alpha = jnp.exp2(m - m_new)
            p = jnp.exp2(s - m_new).astype(BF16)
            pv = jnp.dot(vt_ones[hh], p, preferred_element_type=F32)
            acc_ref[c] = alpha * acc_ref[c] + pv
            m_ref[c] = m_new

    m_ref[...] = jnp.full(m_ref.shape, MASK_VALUE, F32)
    acc_ref[...] = jnp.zeros_like(acc_ref)
    ones_rows = jnp.ones((BF16_SUBLANES, ts), BF16)

    scores(0, 0)

    def step(t, _):
        scores(2 * t + 1, 1)
        softmax_pv(2 * t, 0)
        scores(2 * t + 2, 0)
        softmax_pv(2 * t + 1, 1)
        return 0

    lax.fori_loop(0, qi, step, 0)
    n_full = qi * n_strips
    scores(n_full + 1, 1, first_strip=1)
    softmax_pv(n_full, 0, diag_strip=0)
    softmax_pv(n_full + 1, 1, first_strip=1, diag_strip=1)

    out_t = jnp.concatenate(
        [jnp.concatenate([acc_ref[c, 0:HEAD_DIM, :] / acc_ref[c, HEAD_DIM:HEAD_DIM + 1, :]
                          for c in range(st * 2, st * 2 + 2)], axis=0)
         for st in range(n_strips)], axis=1)
    o_ref[...] = out_t.T.astype(BF16)


def _attention(q, k, vt, cum_t):
    bsz, seq, w = q.shape
    pairs = w // LANES
    tq = T_ATTN
    ts = TS_ATTN
    n_chains = 2 * (tq // ts)
    return pl.pallas_call(
        _attn_kernel,
        name="fox_attn",
        grid=(bsz, pairs, seq // tq),
        in_specs=[
            pl.BlockSpec((None, tq, LANES), lambda b, p, i: (b, i, p)),
            pl.BlockSpec((None, seq, LANES), lambda b, p, i: (b, 0, p)),
            pl.BlockSpec((None, LANES, seq), lambda b, p, i: (b, p, 0)),
            pl.BlockSpec((None, None, seq, 2), lambda b, p, i: (b, p, 0, 0)),
        ],
        out_specs=pl.BlockSpec((None, tq, LANES), lambda b, p, i: (b, i, p)),
        out_shape=jax.ShapeDtypeStruct(q.shape, BF16),
        scratch_shapes=[
            pltpu.VMEM((2, n_chains, ts, ts), F32),
            pltpu.VMEM((2, n_chains, 1, ts), F32),
            pltpu.VMEM((n_chains, 1, ts), F32),
            pltpu.VMEM((n_chains, HEAD_DIM + BF16_SUBLANES, ts), F32),
        ],
        compiler_params=pltpu.CompilerParams(
            dimension_semantics=("arbitrary", "arbitrary", "arbitrary"),
            vmem_limit_bytes=VMEM_LIMIT),
    )(q, k, vt, cum_t)


def _merge_kernel(x_ref, ya_ref, yb_ref, yc_ref, gpre_ref, wg_ref, bg_ref, wbr_ref, wout_ref,
                  gpost_ref, o_ref):
    x = x_ref[...]
    d = x.shape[1]
    hn = _rms(x, gpre_ref[...]).astype(BF16)
    merged = None
    for i, y_ref in enumerate((ya_ref, yb_ref, yc_ref)):
        cols = slice(i * d, (i + 1) * d)
        gate = jax.nn.sigmoid(
            jnp.dot(hn, wg_ref[:, cols], preferred_element_type=F32) + bg_ref[:, cols])
        br = jnp.dot(y_ref[...], wbr_ref[i * BRANCH_WIDTH:(i + 1) * BRANCH_WIDTH, :],
                     preferred_element_type=F32)
        merged = gate * br if merged is None else merged + gate * br
    mix = jnp.dot(merged.astype(BF16), wout_ref[...], preferred_element_type=F32)
    o_ref[...] = x + _rms(mix, gpost_ref[...])


def _merge(layer, x, ya, yb, yc, g_pre, w_g, b_g, w_br, w_out, g_post):
    n, d = x.shape
    tm = TM_PROJ
    row_blk = lambda width: pl.BlockSpec((tm, width), lambda i: (i, 0))
    consts = (g_pre, w_g, b_g, w_br, w_out, g_post)
    return pl.pallas_call(
        _merge_kernel,
        name="merge",
        grid=(n // tm,),
        in_specs=[row_blk(d)] + [row_blk(BRANCH_WIDTH)] * 3 + [_layer_spec(c, layer) for c in consts],
        out_specs=row_blk(d),
        out_shape=jax.ShapeDtypeStruct(x.shape, F32),
        compiler_params=pltpu.CompilerParams(
            dimension_semantics=("arbitrary",),
            vmem_limit_bytes=VMEM_LIMIT),
    )(x, ya, yb, yc, *consts)


def _mlp_kernel(x_ref, gpre_ref, w1_ref, w2_ref, gpost_ref, o_ref):
    x = x_ref[...]
    d = x.shape[1]
    hn = _rms(x, gpre_ref[...]).astype(BF16)
    acc = None
    for c in range(w1_ref.shape[1] // d):
        cols = slice(c * d, (c + 1) * d)
        h1 = jnp.dot(hn, w1_ref[:, cols], preferred_element_type=F32)
        h1 = jnp.square(jnp.maximum(h1, 0.0)).astype(BF16)
        part = jnp.dot(h1, w2_ref[cols, :], preferred_element_type=F32)
        acc = part if acc is None else acc + part
    o_ref[...] = x + _rms(acc, gpost_ref[...])


def _mlp(layer, x, g_pre, w1, w2, g_post):
    n, d = x.shape
    tm = TM_PROJ
    blk = pl.BlockSpec((tm, d), lambda i: (i, 0))
    consts = (g_pre, w1, w2, g_post)
    return pl.pallas_call(
        _mlp_kernel,
        name="mlp",
        grid=(n // tm,),
        in_specs=[blk] + [_layer_spec(c, layer) for c in consts],
        out_specs=blk,
        out_shape=jax.ShapeDtypeStruct(x.shape, F32),
        compiler_params=pltpu.CompilerParams(
            dimension_semantics=("arbitrary",),
            vmem_limit_bytes=VMEM_LIMIT),
    )(x, *consts)


def kernel(x, g_pre_mix, w_in, b_gate, s5_a_re, s5_a_im, s5_log_dt, s5_b_re, s5_b_im, s5_c_re,
           s5_c_im, s5_d, s5_w_glu, conv_w, fox_b_f, w_branch, w_out, g_post_mix, g_pre_mlp,
           w_ff1, w_ff2, g_post_mlp):
    bsz, seq, d = x.shape
    depth = g_pre_mix.shape[0]
    n = bsz * seq

    n_act = 6 * BRANCH_WIDTH
    n_v = n_act + BRANCH_WIDTH
    w_in_b = w_in.astype(BF16)
    wv_t = jnp.swapaxes(w_in_b[:, :, n_act:n_v], 1, 2)
    wf_t = jnp.swapaxes(w_in_b[:, :, n_v:n_v + ATTN_HEADS], 1, 2)
    w_g = w_in_b[:, :, n_v + ATTN_HEADS:]
    w_br, w_o = w_branch.astype(BF16), w_out.astype(BF16)
    w1, w2 = w_ff1.astype(BF16), w_ff2.astype(BF16)
    row = lambda v: v.astype(F32).reshape(depth, 1, -1)
    g_mix, g_post, g_mlp, g_post2, b_g = map(
        row, (g_pre_mix, g_post_mix, g_pre_mlp, g_post_mlp, b_gate))
    b_f = fox_b_f.astype(F32).reshape(depth, ATTN_HEADS, 1)
    cw = conv_w.astype(F32)
    s5_params = _s5_params(s5_a_re, s5_a_im, s5_log_dt, s5_b_re, s5_b_im, s5_c_re, s5_c_im,
                           s5_d, s5_w_glu)

    flat = lambda t: t.reshape(n, t.shape[-1])
    for layer in range(depth):
        u, yb, q, k, vt, cum_t = _in_proj(layer, x, g_mix, w_in_b, wv_t, wf_t, b_f, cw)
        ya = _s5(layer, u, s5_params)
        yc = _attention(q, k, vt, cum_t)
        x1 = _merge(layer, flat(x), flat(ya), flat(yb), flat(yc), g_mix, w_g, b_g, w_br, w_o,
                    g_post)
        x = _mlp(layer, x1, g_mlp, w1, w2, g_post2).reshape(bsz, seq, d)
    return x
```

```python
import functools
import math

import jax
import jax.numpy as jnp
from jax import lax
from jax.experimental import pallas as pl
from jax.experimental.pallas import tpu as pltpu

F32 = jnp.float32
BF16 = jnp.bfloat16

EPS = 1e-6
MASK_VALUE = -1e30

S5_GROUP = 16
S5_STATE = 64
ATTN_HEADS = 8
HEAD_DIM = 64
CONV_K = 3
BRANCH_WIDTH = 512

LANES = 128
SUBLANES = 8
BF16_SUBLANES = 16
MXU_DIM = 256
VMEM_LIMIT = 56 * 1024 * 1024

TM_PROJ = 512
TT_S5 = 64
S5_SUB = 16
T_ATTN = 1024
TS_ATTN = 256
LOG2E = math.log2(math.e)
CUM_PIECES = 3


def _rms(x, g):
    return x * lax.rsqrt(jnp.mean(x * x, axis=-1, keepdims=True) + EPS) * g


def _cumsum_lanes(x):
    n = x.shape[-1]
    lane = lax.broadcasted_iota(jnp.int32, x.shape, x.ndim - 1)
    shift = 1
    while shift < n:
        x = x + jnp.where(lane >= shift, pltpu.roll(x, shift, x.ndim - 1), 0.0)
        shift *= 2
    return x


def _layer_spec(arr, layer, block=None):
    block = arr.shape[1:] if block is None else block
    return pl.BlockSpec((None,) + tuple(block), lambda *_: (layer,) + (0,) * len(block),
                        pipeline_mode=pl.Buffered(1))


def _inproj_kernel(x_ref, g_ref, w_ref, wvt_ref, wft_ref, bf_ref, cw_ref,
                   u_ref, yb_ref, q_ref, k_ref, vt_ref, kc_ref,
                   vtail_ref, carry_ref):
    j = pl.program_id(1)
    tm = x_ref.shape[0]
    w = BRANCH_WIDTH

    @pl.when(j == 0)
    def _():
        vtail_ref[...] = jnp.zeros_like(vtail_ref)
        carry_ref[...] = jnp.zeros_like(carry_ref)

    hn = _rms(x_ref[...], g_ref[...]).astype(BF16)

    def proj(c):
        return jnp.dot(hn, w_ref[:, c * w:(c + 1) * w], preferred_element_type=F32)

    u_ref[...] = proj(0).astype(BF16)

    vv = proj(3) * proj(1)
    tail = vtail_ref[...]
    row = lax.broadcasted_iota(jnp.int32, vv.shape, 0)
    v1 = jnp.where(row == 0, tail[7:8, :], pltpu.roll(vv, 1, 0))
    v2 = jnp.where(row == 0, tail[6:7, :],
                   jnp.where(row == 1, tail[7:8, :], pltpu.roll(vv, 2, 0)))
    cw = cw_ref[...]
    conv = v2 * cw[0:1, :] + v1 * cw[1:2, :] + vv * cw[2:3, :]
    yb_ref[...] = (proj(2) * conv).astype(BF16)
    vtail_ref[...] = vv[tm - SUBLANES:tm, :]

    q_ref[...] = (proj(4) * (HEAD_DIM ** -0.5 * LOG2E)).astype(BF16)
    k_ref[...] = proj(5).astype(BF16)
    nt_dims = (((1,), (1,)), ((), ()))
    vt_ref[...] = lax.dot_general(wvt_ref[...], hn, nt_dims,
                                  preferred_element_type=F32).astype(BF16)

    fl = lax.dot_general(wft_ref[...], hn, nt_dims, preferred_element_type=F32)
    z = fl + bf_ref[...]
    log_f = (jnp.minimum(z, 0.0) - jnp.log1p(jnp.exp(-jnp.abs(z)))) * LOG2E
    cum = _cumsum_lanes(log_f) + carry_ref[:, 0:1]
    carry_ref[...] = jnp.broadcast_to(cum[:, tm - 1:tm], carry_ref.shape)
    hi = cum.astype(BF16).astype(F32)
    mid = (cum - hi).astype(BF16).astype(F32)
    lo = cum - hi - mid
    pieces = jnp.concatenate(
        [hi, mid, lo, jnp.zeros((LANES - CUM_PIECES * ATTN_HEADS, tm), F32)], axis=0)
    kc_ref[...] = pieces.T.astype(BF16)


def _in_proj(layer, x, g, w_in, wv_t, wf_t, b_f, conv_w):
    bsz, seq, d = x.shape
    tm = TM_PROJ
    n_act = 6 * BRANCH_WIDTH
    act = jax.ShapeDtypeStruct((bsz, seq, BRANCH_WIDTH), BF16)
    act_spec = pl.BlockSpec((None, tm, BRANCH_WIDTH), lambda b, j: (b, j, 0))
    return pl.pallas_call(
        _inproj_kernel,
        name="in_proj",
        grid=(bsz, seq // tm),
        in_specs=[
            pl.BlockSpec((None, tm, d), lambda b, j: (b, j, 0)),
            _layer_spec(g, layer),
            _layer_spec(w_in, layer, (d, n_act)),
            _layer_spec(wv_t, layer),
            _layer_spec(wf_t, layer),
            _layer_spec(b_f, layer),
            _layer_spec(conv_w, layer),
        ],
        out_specs=[act_spec] * 4 + [
            pl.BlockSpec((None, BRANCH_WIDTH, tm), lambda b, j: (b, 0, j)),
            pl.BlockSpec((None, tm, LANES), lambda b, j: (b, j, 0))],
        out_shape=[act] * 4 + [
            jax.ShapeDtypeStruct((bsz, BRANCH_WIDTH, seq), BF16),
            jax.ShapeDtypeStruct((bsz, seq, LANES), BF16)],
        scratch_shapes=[pltpu.VMEM((SUBLANES, BRANCH_WIDTH), F32),
                        pltpu.VMEM((ATTN_HEADS, LANES), F32)],
        compiler_params=pltpu.CompilerParams(
            dimension_semantics=("arbitrary", "arbitrary"),
            vmem_limit_bytes=VMEM_LIMIT),
    )(x, g, w_in, wv_t, wf_t, b_f, conv_w)


def _gelu_tanh(x):
    c = math.sqrt(2.0 / math.pi)
    return 0.5 * x * (1.0 + jnp.tanh(c * (x + 0.044715 * (x * x * x))))


def _s5_kernel(u_ref, bre_ref, bim_ref, cre_ref, cim_ref, are_ref, aim_ref, d_ref, wglu_ref,
               y_ref, utb_ref, sre_ref, sim_ref, stre_ref, stim_ref, ytb_ref):
    i = pl.program_id(0)
    bsz, tt, _ = u_ref.shape
    half_in = MXU_DIM
    half_st = sre_ref.shape[1] // 2

    @pl.when(i == 0)
    def _():
        stre_ref[...] = jnp.zeros_like(stre_ref)
        stim_ref[...] = jnp.zeros_like(stim_ref)

    n_ct = utb_ref.shape[0]
    for b in range(bsz):
        ub_f32 = u_ref[b].astype(F32)
        for c in range(n_ct):
            utb_ref[c, pl.ds(b, tt, stride=bsz), :] = ub_f32[:, c * LANES:(c + 1) * LANES]
    sub_rows = S5_SUB * bsz
    tiles_per_half = half_in // LANES

    def rows_of(s):
        return slice(s * sub_rows, (s + 1) * sub_rows)

    def input_proj(s):
        for kt in range(2):
            lhs = jnp.concatenate(
                [utb_ref[kt * tiles_per_half + c, rows_of(s), :] for c in range(tiles_per_half)],
                axis=1).astype(BF16)
            cols = slice(kt * half_st, (kt + 1) * half_st)
            sre_ref[rows_of(s), cols] = jnp.dot(lhs, bre_ref[kt], preferred_element_type=F32)
            sim_ref[rows_of(s), cols] = jnp.dot(lhs, bim_ref[kt], preferred_element_type=F32)

    def scan(s, sr, si):
        for t in range(S5_SUB):
            rows = slice(s * sub_rows + t * bsz, s * sub_rows + (t + 1) * bsz)
            ar = are_ref[...]
            ai = aim_ref[...]
            sr, si = (ar * sr - ai * si + sre_ref[rows, :],
                      ar * si + ai * sr + sim_ref[rows, :])
            sre_ref[rows, :] = sr
            sim_ref[rows, :] = si
        return sr, si

    def output_proj(s):
        for nt in range(2):
            cols = slice(nt * half_st, (nt + 1) * half_st)
            y = jnp.dot(sre_ref[rows_of(s), cols].astype(BF16), cre_ref[nt],
                        preferred_element_type=F32)
            y = y + jnp.dot(sim_ref[rows_of(s), cols].astype(BF16), cim_ref[nt],
                            preferred_element_type=F32)
            u_half = jnp.concatenate(
                [utb_ref[nt * tiles_per_half + c, rows_of(s), :] for c in range(tiles_per_half)],
                axis=1)
            y = _gelu_tanh(y + d_ref[:, nt * half_in:(nt + 1) * half_in] * u_half)
            glu = jnp.dot(y.astype(BF16), wglu_ref[nt], preferred_element_type=F32)
            y = y * jax.nn.sigmoid(glu)
            for c in range(tiles_per_half):
                ytb_ref[nt * tiles_per_half + c, rows_of(s), :] = y[:, c * LANES:(c + 1) * LANES]

    n_sub = tt // S5_SUB
    sr, si = stre_ref[...], stim_ref[...]
    input_proj(0)
    for s in range(n_sub):
        if s + 1 < n_sub:
            input_proj(s + 1)
        sr, si = scan(s, sr, si)
        output_proj(s)
    stre_ref[...] = sr
    stim_ref[...] = si

    for b in range(bsz):
        y_ref[b] = jnp.concatenate(
            [ytb_ref[c, pl.ds(b, tt, stride=bsz), :] for c in range(n_ct)], axis=1).astype(BF16)


def _s5_params(a_re, a_im, log_dt, b_re, b_im, c_re, c_im, d_skip, w_glu):
    depth, g, p, h = b_re.shape
    lam_re = jnp.minimum(a_re.astype(F32), -1e-4)
    lam_im = a_im.astype(F32)
    dt = jnp.exp(log_dt.astype(F32))[..., None]
    mag = jnp.exp(lam_re * dt)
    abar_re = mag * jnp.cos(lam_im * dt)
    abar_im = mag * jnp.sin(lam_im * dt)
    inv = 1.0 / (lam_re * lam_re + lam_im * lam_im)
    coef_re = (((abar_re - 1.0) * lam_re + abar_im * lam_im) * inv)[..., None]
    coef_im = ((abar_im * lam_re - (abar_re - 1.0) * lam_im) * inv)[..., None]
    bbar_re = coef_re * b_re.astype(F32) - coef_im * b_im.astype(F32)
    bbar_im = coef_re * b_im.astype(F32) + coef_im * b_re.astype(F32)
    gb = g // 2
    eye = jnp.eye(gb, dtype=BF16)

    def blocks(m, spec, rows, cols):
        m = m.astype(BF16).reshape(depth, 2, gb, m.shape[2], m.shape[3])
        return jnp.einsum(spec, m, eye).reshape(depth, 2, rows, cols)

    bre = blocks(bbar_re, 'dkgph,gj->dkghjp', gb * h, gb * p)
    bim = blocks(bbar_im, 'dkgph,gj->dkghjp', gb * h, gb * p)
    cre = blocks(c_re, 'dkghp,gj->dkgpjh', gb * p, gb * h)
    cim = blocks(-c_im.astype(F32), 'dkghp,gj->dkgpjh', gb * p, gb * h)
    wglu = blocks(w_glu, 'dkghc,gj->dkghjc', gb * h, gb * h)
    are = jnp.broadcast_to(abar_re.reshape(depth, 1, g * p), (depth, SUBLANES, g * p))
    aim = jnp.broadcast_to(abar_im.reshape(depth, 1, g * p), (depth, SUBLANES, g * p))
    return bre, bim, cre, cim, are, aim, d_skip.astype(F32).reshape(depth, 1, g * h), wglu


def _s5(layer, u, params):
    bsz, seq, w = u.shape
    consts = params
    n_state = params[4].shape[-1]
    tt = TT_S5
    blk = pl.BlockSpec((bsz, tt, w), lambda i: (0, i, 0))
    return pl.pallas_call(
        _s5_kernel,
        name="s5",
        grid=(seq // tt,),
        in_specs=[blk] + [_layer_spec(c, layer) for c in consts],
        out_specs=blk,
        out_shape=jax.ShapeDtypeStruct(u.shape, BF16),
        scratch_shapes=[
            pltpu.VMEM((w // LANES, tt * bsz, LANES), F32),
            pltpu.VMEM((tt * bsz, n_state), F32),
            pltpu.VMEM((tt * bsz, n_state), F32),
            pltpu.VMEM((bsz, n_state), F32),
            pltpu.VMEM((bsz, n_state), F32),
            pltpu.VMEM((w // LANES, tt * bsz, LANES), F32),
        ],
        compiler_params=pltpu.CompilerParams(
            dimension_semantics=("arbitrary",),
            vmem_limit_bytes=VMEM_LIMIT),
    )(u, *consts)


def _attn_kernel(q_ref, k_ref, kc_ref, vt_ref, o_ref, s_ref, bmax_ref, m_ref, acc_ref):
    pair = pl.program_id(1)
    qi = pl.program_id(2)
    tq = q_ref.shape[0]
    ts = TS_ATTN
    n_strips = tq // ts
    assert n_strips % 2 == 0
    q = q_ref[...]
    lane = lax.broadcasted_iota(jnp.int32, (ts, LANES), 1)
    k_pos = lax.broadcasted_iota(jnp.int32, (ts, ts), 0)
    q_pos = lax.broadcasted_iota(jnp.int32, (ts, ts), 1)
    nt_dims = (((1,), (1,)), ((), ()))

    chains = [(st, hh) for st in range(n_strips) for hh in range(2)]
    qs = {}
    for st, hh in chains:
        q_st = q[st * ts:(st + 1) * ts, :]
        head_lanes = (lane >= HEAD_DIM) if hh else (lane < HEAD_DIM)
        piece_lanes = (lane < CUM_PIECES * ATTN_HEADS) & (lane % ATTN_HEADS == 2 * pair + hh)
        qs[st, hh] = jnp.concatenate(
            [jnp.where(head_lanes, q_st, jnp.zeros_like(q_st)),
             jnp.where(piece_lanes, -1.0, 0.0).astype(BF16)], axis=1)

    def key_start(blk):
        return pl.multiple_of(blk * ts, ts)

    def scores(blk, buf, first_strip=0):
        k0 = key_start(blk)
        kb = jnp.concatenate([k_ref[pl.ds(k0, ts), :], kc_ref[pl.ds(k0, ts), :]], axis=1)
        for c, (st, hh) in enumerate(chains):
            if st < first_strip:
                continue
            s = lax.dot_general(kb, qs[st, hh], nt_dims, preferred_element_type=F32)
            s_ref[buf, c] = s
            bmax_ref[buf, c] = jnp.max(s, axis=0, keepdims=True)

    def softmax_pv(blk, buf, first_strip=0, diag_strip=None):
        vt = vt_ref[:, pl.ds(key_start(blk), ts)]
        vt_ones = [jnp.concatenate([vt[hh * HEAD_DIM:(hh + 1) * HEAD_DIM, :], ones_rows], axis=0)
                   for hh in range(2)]
        for c, (st, hh) in enumerate(chains):
            if st < first_strip:
                continue
            s = s_ref[buf, c]
            blk_max = bmax_ref[buf, c]
            if st == diag_strip:
                s = jnp.where(k_pos <= q_pos, s, MASK_VALUE)
                blk_max = jnp.max(s, axis=0, keepdims=True)
            m = m_ref[c]
            m_new = jnp.maximum(m, blk_max)
            alpha = jnp.exp2(m - m_new)
            p = jnp.exp2(s - m_new).astype(BF16)
            pv = jnp.dot(vt_ones[hh], p, preferred_element_type=F32)
            acc_ref[c] = alpha * acc_ref[c] + pv
            m_ref[c] = m_new

    m_ref[...] = jnp.full(m_ref.shape, MASK_VALUE, F32)
    acc_ref[...] = jnp.zeros_like(acc_ref)
    ones_rows = jnp.ones((BF16_SUBLANES, ts), BF16)

    scores(0, 0)

    def step(t, _):
        scores(2 * t + 1, 1)
        softmax_pv(2 * t, 0)
        scores(2 * t + 2, 0)
        softmax_pv(2 * t + 1, 1)
        return 0

    n_full = qi * n_strips
    lax.fori_loop(0, qi * (n_strips // 2), step, 0)
    for d in range(n_strips):
        if d + 1 < n_strips:
            scores(n_full + d + 1, (d + 1) % 2, first_strip=d + 1)
        softmax_pv(n_full + d, d % 2, first_strip=d, diag_strip=d)

    out_t = jnp.concatenate(
        [jnp.concatenate([acc_ref[c, 0:HEAD_DIM, :] / acc_ref[c, HEAD_DIM:HEAD_DIM + 1, :]
                          for c in range(st * 2, st * 2 + 2)], axis=0)
         for st in range(n_strips)], axis=1)
    o_ref[...] = out_t.T.astype(BF16)


def _attention(q, k, kc, vt):
    bsz, seq, w = q.shape
    pairs = w // LANES
    tq = T_ATTN
    ts = TS_ATTN
    n_chains = 2 * (tq // ts)
    return pl.pallas_call(
        _attn_kernel,
        name="fox_attn",
        grid=(bsz, pairs, seq // tq),
        in_specs=[
            pl.BlockSpec((None, tq, LANES), lambda b, p, i: (b, i, p)),
            pl.BlockSpec((None, seq, LANES), lambda b, p, i: (b, 0, p)),
            pl.BlockSpec((None, seq, LANES), lambda b, p, i: (b, 0, 0)),
            pl.BlockSpec((None, LANES, seq), lambda b, p, i: (b, p, 0)),
        ],
        out_specs=pl.BlockSpec((None, tq, LANES), lambda b, p, i: (b, i, p)),
        out_shape=jax.ShapeDtypeStruct(q.shape, BF16),
        scratch_shapes=[
            pltpu.VMEM((2, n_chains, ts, ts), F32),
            pltpu.VMEM((2, n_chains, 1, ts), F32),
            pltpu.VMEM((n_chains, 1, ts), F32),
            pltpu.VMEM((n_chains, HEAD_DIM + BF16_SUBLANES, ts), F32),
        ],
        compiler_params=pltpu.CompilerParams(
            dimension_semantics=("arbitrary", "arbitrary", "arbitrary"),
            vmem_limit_bytes=VMEM_LIMIT),
    )(q, k, kc, vt)


def _merge_kernel(x_ref, ya_ref, yb_ref, yc_ref, gpre_ref, wg_ref, bg_ref, wbr_ref, wout_ref,
                  gpost_ref, o_ref):
    x = x_ref[...]
    d = x.shape[1]
    hn = _rms(x, gpre_ref[...]).astype(BF16)
    merged = None
    for i, y_ref in enumerate((ya_ref, yb_ref, yc_ref)):
        cols = slice(i * d, (i + 1) * d)
        gate = jax.nn.sigmoid(
            jnp.dot(hn, wg_ref[:, cols], preferred_element_type=F32) + bg_ref[:, cols])
        br = jnp.dot(y_ref[...], wbr_ref[i * BRANCH_WIDTH:(i + 1) * BRANCH_WIDTH, :],
                     preferred_element_type=F32)
        merged = gate * br if merged is None else merged + gate * br
    mix = jnp.dot(merged.astype(BF16), wout_ref[...], preferred_element_type=F32)
    o_ref[...] = x + _rms(mix, gpost_ref[...])


def _merge(layer, x, ya, yb, yc, g_pre, w_g, b_g, w_br, w_out, g_post):
    n, d = x.shape
    tm = TM_PROJ
    row_blk = lambda width: pl.BlockSpec((tm, width), lambda i: (i, 0))
    consts = (g_pre, w_g, b_g, w_br, w_out, g_post)
    return pl.pallas_call(
        _merge_kernel,
        name="merge",
        grid=(n // tm,),
        in_specs=[row_blk(d)] + [row_blk(BRANCH_WIDTH)] * 3 + [_layer_spec(c, layer) for c in consts],
        out_specs=row_blk(d),
        out_shape=jax.ShapeDtypeStruct(x.shape, F32),
        compiler_params=pltpu.CompilerParams(
            dimension_semantics=("arbitrary",),
            vmem_limit_bytes=VMEM_LIMIT),
    )(x, ya, yb, yc, *consts)


def _mlp_kernel(x_ref, gpre_ref, w1_ref, w2_ref, gpost_ref, o_ref):
    x = x_ref[...]
    d = x.shape[1]
    hn = _rms(x, gpre_ref[...]).astype(BF16)
    acc = None
    for c in range(w1_ref.shape[1] // d):
        cols = slice(c * d, (c + 1) * d)
        h1 = jnp.dot(hn, w1_ref[:, cols], preferred_element_type=F32)
        h1 = jnp.square(jnp.maximum(h1, 0.0)).astype(BF16)
        part = jnp.dot(h1, w2_ref[cols, :], preferred_element_type=F32)
        acc = part if acc is None else acc + part
    o_ref[...] = x + _rms(acc, gpost_ref[...])


def _mlp(layer, x, g_pre, w1, w2, g_post):
    n, d = x.shape
    tm = TM_PROJ
    blk = pl.BlockSpec((tm, d), lambda i: (i, 0))
    consts = (g_pre, w1, w2, g_post)
    return pl.pallas_call(
        _mlp_kernel,
        name="mlp",
        grid=(n // tm,),
        in_specs=[blk] + [_layer_spec(c, layer) for c in consts],
        out_specs=blk,
        out_shape=jax.ShapeDtypeStruct(x.shape, F32),
        compiler_params=pltpu.CompilerParams(
            dimension_semantics=("arbitrary",),
            vmem_limit_bytes=VMEM_LIMIT),
    )(x, *consts)


def kernel(x, g_pre_mix, w_in, b_gate, s5_a_re, s5_a_im, s5_log_dt, s5_b_re, s5_b_im, s5_c_re,
           s5_c_im, s5_d, s5_w_glu, conv_w, fox_b_f, w_branch, w_out, g_post_mix, g_pre_mlp,
           w_ff1, w_ff2, g_post_mlp):
    bsz, seq, d = x.shape
    depth = g_pre_mix.shape[0]
    n = bsz * seq

    n_act = 6 * BRANCH_WIDTH
    n_v = n_act + BRANCH_WIDTH
    w_in_b = w_in.astype(BF16)
    wv_t = jnp.swapaxes(w_in_b[:, :, n_act:n_v], 1, 2)
    wf_t = jnp.swapaxes(w_in_b[:, :, n_v:n_v + ATTN_HEADS], 1, 2)
    w_g = w_in_b[:, :, n_v + ATTN_HEADS:]
    w_br, w_o = w_branch.astype(BF16), w_out.astype(BF16)
    w1, w2 = w_ff1.astype(BF16), w_ff2.astype(BF16)
    row = lambda v: v.astype(F32).reshape(depth, 1, -1)
    g_mix, g_post, g_mlp, g_post2, b_g = map(
        row, (g_pre_mix, g_post_mix, g_pre_mlp, g_post_mlp, b_gate))
    b_f = fox_b_f.astype(F32).reshape(depth, ATTN_HEADS, 1)
    cw = conv_w.astype(F32)
    s5_params = _s5_params(s5_a_re, s5_a_im, s5_log_dt, s5_b_re, s5_b_im, s5_c_re, s5_c_im,
                           s5_d, s5_w_glu)

    flat = lambda t: t.reshape(n, t.shape[-1])
    for layer in range(depth):
        u, yb, q, k, vt, kc = _in_proj(layer, x, g_mix, w_in_b, wv_t, wf_t, b_f, cw)
        ya = _s5(layer, u, s5_params)
        yc = _attention(q, k, kc, vt)
        x1 = _merge(layer, flat(x), flat(ya), flat(yb), flat(yc), g_mix, w_g, b_g, w_br, w_o,
                    g_post)
        x = _mlp(layer, x1, g_mlp, w1, w2, g_post2).reshape(bsz, seq, d)
    return x
```

```python
import math

import jax
import jax.numpy as jnp
from jax import lax
from jax.experimental import pallas as pl
from jax.experimental.pallas import tpu as pltpu

F32 = jnp.float32
BF16 = jnp.bfloat16

EPS = 1e-6
MASK_VALUE = -1e30

ATTN_HEADS = 8
HEAD_DIM = 64
CONV_K = 3
BRANCH_WIDTH = 512

LANES = 128
SUBLANES = 8
BF16_SUBLANES = 16
VMEM_LIMIT = 56 * 1024 * 1024

TM_PROJ = 512
S5_T = 16
S5_GROUPS_PER_STEP = 8
S5_ROW_BLOCK = 512
T_ATTN = 1024
TS_ATTN = 256
LOG2E = math.log2(math.e)
CUM_PIECES = 3


def _rms(x, g):
    return x * lax.rsqrt(jnp.mean(x * x, axis=-1, keepdims=True) + EPS) * g


def _cumsum_lanes(x):
    n = x.shape[-1]
    lane = lax.broadcasted_iota(jnp.int32, x.shape, x.ndim - 1)
    shift = 1
    while shift < n:
        x = x + jnp.where(lane >= shift, pltpu.roll(x, shift, x.ndim - 1), 0.0)
        shift *= 2
    return x


def _layer_spec(arr, layer, block=None):
    block = arr.shape[1:] if block is None else block
    return pl.BlockSpec((None,) + tuple(block), lambda *_: (layer,) + (0,) * len(block),
                        pipeline_mode=pl.Buffered(1))


def _inproj_kernel(x_ref, g_ref, w_ref, wvt_ref, wft_ref, bf_ref, cw_ref,
                   u_ref, yb_ref, q_ref, k_ref, vt_ref, kc_ref,
                   vtail_ref, carry_ref):
    j = pl.program_id(1)
    tm = x_ref.shape[0]
    w = BRANCH_WIDTH

    @pl.when(j == 0)
    def _():
        vtail_ref[...] = jnp.zeros_like(vtail_ref)
        carry_ref[...] = jnp.zeros_like(carry_ref)

    hn = _rms(x_ref[...], g_ref[...]).astype(BF16)

    def proj(c):
        return jnp.dot(hn, w_ref[:, c * w:(c + 1) * w], preferred_element_type=F32)

    u_ref[...] = proj(0).astype(BF16)

    vv = proj(3) * proj(1)
    tail = vtail_ref[...]
    row = lax.broadcasted_iota(jnp.int32, vv.shape, 0)
    v1 = jnp.where(row == 0, tail[7:8, :], pltpu.roll(vv, 1, 0))
    v2 = jnp.where(row == 0, tail[6:7, :],
                   jnp.where(row == 1, tail[7:8, :], pltpu.roll(vv, 2, 0)))
    cw = cw_ref[...]
    conv = v2 * cw[0:1, :] + v1 * cw[1:2, :] + vv * cw[2:3, :]
    yb_ref[...] = (proj(2) * conv).astype(BF16)
    vtail_ref[...] = vv[tm - SUBLANES:tm, :]

    q_ref[...] = (proj(4) * (HEAD_DIM ** -0.5 * LOG2E)).astype(BF16)
    k_ref[...] = proj(5).astype(BF16)
    nt_dims = (((1,), (1,)), ((), ()))
    vt_ref[...] = lax.dot_general(wvt_ref[...], hn, nt_dims,
                                  preferred_element_type=F32).astype(BF16)

    fl = lax.dot_general(wft_ref[...], hn, nt_dims, preferred_element_type=F32)
    z = fl + bf_ref[...]
    log_f = (jnp.minimum(z, 0.0) - jnp.log1p(jnp.exp(-jnp.abs(z)))) * LOG2E
    cum = _cumsum_lanes(log_f) + carry_ref[:, 0:1]
    carry_ref[...] = jnp.broadcast_to(cum[:, tm - 1:tm], carry_ref.shape)
    hi = cum.astype(BF16).astype(F32)
    mid = (cum - hi).astype(BF16).astype(F32)
    lo = cum - hi - mid
    pieces = jnp.concatenate(
        [hi, mid, lo, jnp.zeros((LANES - CUM_PIECES * ATTN_HEADS, tm), F32)], axis=0)
    kc_ref[...] = pieces.T.astype(BF16)


def _in_proj(layer, x, g, w_in, wv_t, wf_t, b_f, conv_w):
    bsz, seq, d = x.shape
    tm = TM_PROJ
    n_act = 6 * BRANCH_WIDTH
    act = jax.ShapeDtypeStruct((bsz, seq, BRANCH_WIDTH), BF16)
    act_spec = pl.BlockSpec((None, tm, BRANCH_WIDTH), lambda b, j: (b, j, 0))
    return pl.pallas_call(
        _inproj_kernel,
        name="in_proj",
        grid=(bsz, seq // tm),
        in_specs=[
            pl.BlockSpec((None, tm, d), lambda b, j: (b, j, 0)),
            _layer_spec(g, layer),
            _layer_spec(w_in, layer, (d, n_act)),
            _layer_spec(wv_t, layer),
            _layer_spec(wf_t, layer),
            _layer_spec(b_f, layer),
            _layer_spec(conv_w, layer),
        ],
        out_specs=[act_spec] * 4 + [
            pl.BlockSpec((None, BRANCH_WIDTH, tm), lambda b, j: (b, 0, j)),
            pl.BlockSpec((None, tm, LANES), lambda b, j: (b, j, 0))],
        out_shape=[act] * 4 + [
            jax.ShapeDtypeStruct((bsz, BRANCH_WIDTH, seq), BF16),
            jax.ShapeDtypeStruct((bsz, seq, LANES), BF16)],
        scratch_shapes=[pltpu.VMEM((SUBLANES, BRANCH_WIDTH), F32),
                        pltpu.VMEM((ATTN_HEADS, LANES), F32)],
        compiler_params=pltpu.CompilerParams(
            dimension_semantics=("arbitrary", "arbitrary"),
            vmem_limit_bytes=VMEM_LIMIT),
    )(x, g, w_in, wv_t, wf_t, b_f, conv_w)


def _gelu_tanh(x):
    c = math.sqrt(2.0 / math.pi)
    return 0.5 * x * (1.0 + jnp.tanh(c * (x + 0.044715 * (x * x * x))))


def _s5_kernel(u_ref, m_ref, wre_ref, wim_ref, vre_ref, vim_ref, are_ref, aim_ref, d_ref, glu_ref,
               y_ref, xre_ref, xim_ref):
    gb, rows, _ = u_ref.shape
    n_pairs = gb // 2
    bsz = SUBLANES
    n_chunks = rows // bsz

    for pr in range(n_pairs):
        u_pair = jnp.concatenate([u_ref[2 * pr], u_ref[2 * pr + 1]], axis=1)
        cols = slice(pr * LANES, (pr + 1) * LANES)
        xre_ref[:, cols] = jnp.dot(u_pair, wre_ref[pr], preferred_element_type=F32)
        xim_ref[:, cols] = jnp.dot(u_pair, wim_ref[pr], preferred_element_type=F32)

    def step(c, carry):
        sr, si = carry
        r = pl.ds(pl.multiple_of(c * bsz, bsz), bsz)
        xr = xre_ref[r, :]
        xi = xim_ref[r, :]
        xre_ref[r, :] = sr
        xim_ref[r, :] = si
        ar = are_ref[...]
        ai = aim_ref[...]
        return ar * sr - ai * si + xr, ar * si + ai * sr + xi

    zero = jnp.zeros((bsz, n_pairs * LANES), F32)
    lax.fori_loop(0, n_chunks, step, (zero, zero), unroll=4)

    for g in range(gb):
        cols = slice((g // 2) * LANES, (g // 2 + 1) * LANES)
        for r0 in range(0, rows, S5_ROW_BLOCK):
            r = slice(r0, r0 + S5_ROW_BLOCK)
            u = u_ref[g, r, :]
            y = jnp.dot(u, m_ref[g], preferred_element_type=F32)
            y = y + jnp.dot(xre_ref[r, cols].astype(BF16), vre_ref[g], preferred_element_type=F32)
            y = y + jnp.dot(xim_ref[r, cols].astype(BF16), vim_ref[g], preferred_element_type=F32)
            y = _gelu_tanh(y + d_ref[g] * u.astype(F32))
            gate = jnp.dot(y.astype(BF16), glu_ref[g], preferred_element_type=F32)
            y_ref[g, r, :] = (y * jax.nn.sigmoid(gate)).astype(BF16)


def _s5_params(a_re, a_im, log_dt, b_re, b_im, c_re, c_im, d_skip, w_glu):
    depth, g, p, h = b_re.shape
    t_sub = S5_T
    f = lambda v: v.astype(F32)
    lam_re = jnp.minimum(f(a_re), -1e-4)
    lam_im = f(a_im)
    dt = jnp.exp(f(log_dt))[..., None]
    rho, theta = lam_re * dt, lam_im * dt

    abar_re = jnp.exp(rho) * jnp.cos(theta)
    abar_im = jnp.exp(rho) * jnp.sin(theta)
    inv = 1.0 / (lam_re * lam_re + lam_im * lam_im)
    coef_re = (((abar_re - 1.0) * lam_re + abar_im * lam_im) * inv)[..., None]
    coef_im = ((abar_im * lam_re - (abar_re - 1.0) * lam_im) * inv)[..., None]
    bbar_re = coef_re * f(b_re) - coef_im * f(b_im)
    bbar_im = coef_re * f(b_im) + coef_im * f(b_re)

    tau = jnp.arange(t_sub + 1, dtype=F32)[:, None]
    pw_mag = jnp.exp(tau * rho[:, :, None, :])
    pw_re = pw_mag * jnp.cos(tau * theta[:, :, None, :])
    pw_im = pw_mag * jnp.sin(tau * theta[:, :, None, :])

    pr, pi = pw_re[:, :, :t_sub, :, None], pw_im[:, :, :t_sub, :, None]
    ab_re = pr * bbar_re[:, :, None] - pi * bbar_im[:, :, None]
    ab_im = pr * bbar_im[:, :, None] + pi * bbar_re[:, :, None]

    kern = (jnp.einsum('dgop,dgtph->dgtoh', f(c_re), ab_re)
            - jnp.einsum('dgop,dgtph->dgtoh', f(c_im), ab_im))
    lag = jnp.arange(t_sub)[None, :] - jnp.arange(t_sub)[:, None]
    m = jnp.where((lag >= 0)[None, None, :, :, None, None],
                  kern[:, :, jnp.clip(lag, 0, t_sub - 1)], 0.0)
    m = m.transpose(0, 1, 2, 5, 3, 4).reshape(depth, g, t_sub * h, t_sub * h)

    def pair_block_diag(w):
        w = w[:, :, ::-1].transpose(0, 1, 2, 4, 3).reshape(depth, g // 2, 2, t_sub * h, p)
        return jnp.einsum('dqirc,ij->dqirjc', w, jnp.eye(2, dtype=F32)).reshape(
            depth, g // 2, 2 * t_sub * h, 2 * p)

    w_re, w_im = pair_block_diag(ab_re), pair_block_diag(ab_im)

    p1_re, p1_im = pw_re[:, :, 1:, None, :], pw_im[:, :, 1:, None, :]
    ca_re = f(c_re)[:, :, None] * p1_re - f(c_im)[:, :, None] * p1_im
    ca_im = f(c_re)[:, :, None] * p1_im + f(c_im)[:, :, None] * p1_re

    def pair_rows(v):
        v = v.transpose(0, 1, 4, 2, 3).reshape(depth, g // 2, 2, p, t_sub * h)
        return jnp.einsum('dqipc,ij->dqijpc', v, jnp.eye(2, dtype=F32)).reshape(
            depth, g, 2 * p, t_sub * h)

    v_re, v_im = pair_rows(ca_re), pair_rows(-ca_im)

    at_re = jnp.broadcast_to(pw_re[:, :, t_sub].reshape(depth, 1, g * p), (depth, SUBLANES, g * p))
    at_im = jnp.broadcast_to(pw_im[:, :, t_sub].reshape(depth, 1, g * p), (depth, SUBLANES, g * p))

    d_t = jnp.tile(f(d_skip)[:, :, None, :], (1, 1, t_sub, 1)).reshape(depth, g, 1, t_sub * h)
    glu = jnp.einsum('dghk,ij->dgihjk', f(w_glu), jnp.eye(t_sub, dtype=F32)).reshape(
        depth, g, t_sub * h, t_sub * h)
    bf = lambda v: v.astype(BF16)
    return bf(m), bf(w_re), bf(w_im), bf(v_re), bf(v_im), at_re, at_im, d_t, bf(glu)


def _s5(layer, u, params):
    bsz, seq, w = u.shape
    m, w_re, w_im, v_re, v_im, at_re, at_im, d_t, glu = params
    g = m.shape[1]
    h = w // g
    t_sub = S5_T
    n_chunks = seq // t_sub
    rows = n_chunks * bsz
    gb = S5_GROUPS_PER_STEP
    assert bsz == SUBLANES and t_sub * h == 2 * LANES
    u_t = u.reshape(bsz, n_chunks, t_sub, g, h).transpose(3, 1, 0, 2, 4).reshape(g, rows, t_sub * h)

    def per_group(arr, groups):
        blk = (None, groups) + arr.shape[2:]
        return pl.BlockSpec(blk, lambda i: (layer, i) + (0,) * (arr.ndim - 2))

    state_w = gb // 2 * LANES
    decay_spec = pl.BlockSpec((None, SUBLANES, state_w), lambda i: (layer, 0, i))
    act_spec = pl.BlockSpec((gb, rows, t_sub * h), lambda i: (i, 0, 0))
    y_t = pl.pallas_call(
        _s5_kernel,
        name="s5",
        grid=(g // gb,),
        in_specs=[act_spec, per_group(m, gb), per_group(w_re, gb // 2), per_group(w_im, gb // 2),
                  per_group(v_re, gb), per_group(v_im, gb), decay_spec, decay_spec,
                  per_group(d_t, gb), per_group(glu, gb)],
        out_specs=act_spec,
        out_shape=jax.ShapeDtypeStruct((g, rows, t_sub * h), BF16),
        scratch_shapes=[pltpu.VMEM((rows, state_w), F32), pltpu.VMEM((rows, state_w), F32)],
        compiler_params=pltpu.CompilerParams(
            dimension_semantics=("arbitrary",),
            vmem_limit_bytes=VMEM_LIMIT),
    )(u_t, m, w_re, w_im, v_re, v_im, at_re, at_im, d_t, glu)
    return y_t.reshape(g, n_chunks, bsz, t_sub, h).transpose(2, 1, 3, 0, 4).reshape(bsz, seq, w)


def _attn_kernel(q_ref, k_ref, kc_ref, vt_ref, o_ref, s_ref, bmax_ref, m_ref, acc_ref):
    pair = pl.program_id(1)
    qi = pl.program_id(2)
    tq = q_ref.shape[0]
    ts = TS_ATTN
    n_strips = tq // ts
    assert n_strips % 2 == 0
    q = q_ref[...]
    lane = lax.broadcasted_iota(jnp.int32, (ts, LANES), 1)
    k_pos = lax.broadcasted_iota(jnp.int32, (ts, ts), 0)
    q_pos = lax.broadcasted_iota(jnp.int32, (ts, ts), 1)
    nt_dims = (((1,), (1,)), ((), ()))

    chains = [(st, hh) for st in range(n_strips) for hh in range(2)]
    qs = {}
    for st, hh in chains:
        q_st = q[st * ts:(st + 1) * ts, :]
        head_lanes = (lane >= HEAD_DIM) if hh else (lane < HEAD_DIM)
        piece_lanes = (lane < CUM_PIECES * ATTN_HEADS) & (lane % ATTN_HEADS == 2 * pair + hh)
        qs[st, hh] = jnp.concatenate(
            [jnp.where(head_lanes, q_st, jnp.zeros_like(q_st)),
             jnp.where(piece_lanes, -1.0, 0.0).astype(BF16)], axis=1)

    def key_start(blk):
        return pl.multiple_of(blk * ts, ts)

    def scores(blk, buf, first_strip=0):
        k0 = key_start(blk)
        kb = jnp.concatenate([k_ref[pl.ds(k0, ts), :], kc_ref[pl.ds(k0, ts), :]], axis=1)
        for c, (st, hh) in enumerate(chains):
            if st < first_strip:
                continue
            s = lax.dot_general(kb, qs[st, hh], nt_dims, preferred_element_type=F32)
            s_ref[buf, c] = s
            bmax_ref[buf, c] = jnp.max(s, axis=0, keepdims=True)

    def softmax_pv(blk, buf, first_strip=0, diag_strip=None):
        vt = vt_ref[:, pl.ds(key_start(blk), ts)]
        vt_ones = [jnp.concatenate([vt[hh * HEAD_DIM:(hh + 1) * HEAD_DIM, :], ones_rows], axis=0)
                   for hh in range(2)]
        for c, (st, hh) in enumerate(chains):
            if st < first_strip:
                continue
            s = s_ref[buf, c]
            blk_max = bmax_ref[buf, c]
            if st == diag_strip:
                s = jnp.where(k_pos <= q_pos, s, MASK_VALUE)
                blk_max = jnp.max(s, axis=0, keepdims=True)
            m = m_ref[c]
            m_new = jnp.maximum(m, blk_max)
            alpha = jnp.exp2(m - m_new)
            p = jnp.exp2(s - m_new).astype(BF16)
            pv = jnp.dot(vt_ones[hh], p, preferred_element_type=F32)
            acc_ref[c] = alpha * acc_ref[c] + pv
            m_ref[c] = m_new

    m_ref[...] = jnp.full(m_ref.shape, MASK_VALUE, F32)
    acc_ref[...] = jnp.zeros_like(acc_ref)
    ones_rows = jnp.ones((BF16_SUBLANES, ts), BF16)

    scores(0, 0)

    def step(t, _):
        scores(2 * t + 1, 1)
        softmax_pv(2 * t, 0)
        scores(2 * t + 2, 0)
        softmax_pv(2 * t + 1, 1)
        return 0

    n_full = qi * n_strips
    lax.fori_loop(0, qi * (n_strips // 2), step, 0)
    for d in range(n_strips):
        if d + 1 < n_strips:
            scores(n_full + d + 1, (d + 1) % 2, first_strip=d + 1)
        softmax_pv(n_full + d, d % 2, first_strip=d, diag_strip=d)

    out_t = jnp.concatenate(
        [jnp.concatenate([acc_ref[c, 0:HEAD_DIM, :] / acc_ref[c, HEAD_DIM:HEAD_DIM + 1, :]
                          for c in range(st * 2, st * 2 + 2)], axis=0)
         for st in range(n_strips)], axis=1)
    o_ref[...] = out_t.T.astype(BF16)


def _attention(q, k, kc, vt):
    bsz, seq, w = q.shape
    pairs = w // LANES
    tq = T_ATTN
    ts = TS_ATTN
    n_chains = 2 * (tq // ts)
    return pl.pallas_call(
        _attn_kernel,
        name="fox_attn",
        grid=(bsz, pairs, seq // tq),
        in_specs=[
            pl.BlockSpec((None, tq, LANES), lambda b, p, i: (b, i, p)),
            pl.BlockSpec((None, seq, LANES), lambda b, p, i: (b, 0, p)),
            pl.BlockSpec((None, seq, LANES), lambda b, p, i: (b, 0, 0)),
            pl.BlockSpec((None, LANES, seq), lambda b, p, i: (b, p, 0)),
        ],
        out_specs=pl.BlockSpec((None, tq, LANES), lambda b, p, i: (b, i, p)),
        out_shape=jax.ShapeDtypeStruct(q.shape, BF16),
        scratch_shapes=[
            pltpu.VMEM((2, n_chains, ts, ts), F32),
            pltpu.VMEM((2, n_chains, 1, ts), F32),
            pltpu.VMEM((n_chains, 1, ts), F32),
            pltpu.VMEM((n_chains, HEAD_DIM + BF16_SUBLANES, ts), F32),
        ],
        compiler_params=pltpu.CompilerParams(
            dimension_semantics=("arbitrary", "arbitrary", "arbitrary"),
            vmem_limit_bytes=VMEM_LIMIT),
    )(q, k, kc, vt)


def _merge_kernel(x_ref, ya_ref, yb_ref, yc_ref, gpre_ref, wg_ref, bg_ref, wbr_ref, wout_ref,
                  gpost_ref, o_ref):
    x = x_ref[...]
    d = x.shape[1]
    hn = _rms(x, gpre_ref[...]).astype(BF16)
    merged = None
    for i, y_ref in enumerate((ya_ref, yb_ref, yc_ref)):
        cols = slice(i * d, (i + 1) * d)
        gate = jax.nn.sigmoid(
            jnp.dot(hn, wg_ref[:, cols], preferred_element_type=F32) + bg_ref[:, cols])
        br = jnp.dot(y_ref[...], wbr_ref[i * BRANCH_WIDTH:(i + 1) * BRANCH_WIDTH, :],
                     preferred_element_type=F32)
        merged = gate * br if merged is None else merged + gate * br
    mix = jnp.dot(merged.astype(BF16), wout_ref[...], preferred_element_type=F32)
    o_ref[...] = x + _rms(mix, gpost_ref[...])


def _merge(layer, x, ya, yb, yc, g_pre, w_g, b_g, w_br, w_out, g_post):
    n, d = x.shape
    tm = TM_PROJ
    row_blk = lambda width: pl.BlockSpec((tm, width), lambda i: (i, 0))
    consts = (g_pre, w_g, b_g, w_br, w_out, g_post)
    return pl.pallas_call(
        _merge_kernel,
        name="merge",
        grid=(n // tm,),
        in_specs=[row_blk(d)] + [row_blk(BRANCH_WIDTH)] * 3 + [_layer_spec(c, layer) for c in consts],
        out_specs=row_blk(d),
        out_shape=jax.ShapeDtypeStruct(x.shape, F32),
        compiler_params=pltpu.CompilerParams(
            dimension_semantics=("arbitrary",),
            vmem_limit_bytes=VMEM_LIMIT),
    )(x, ya, yb, yc, *consts)


def _mlp_kernel(x_ref, gpre_ref, w1_ref, w2_ref, gpost_ref, o_ref):
    x = x_ref[...]
    d = x.shape[1]
    hn = _rms(x, gpre_ref[...]).astype(BF16)
    acc = None
    for c in range(w1_ref.shape[1] // d):
        cols = slice(c * d, (c + 1) * d)
        h1 = jnp.dot(hn, w1_ref[:, cols], preferred_element_type=F32)
        h1 = jnp.square(jnp.maximum(h1, 0.0)).astype(BF16)
        part = jnp.dot(h1, w2_ref[cols, :], preferred_element_type=F32)
        acc = part if acc is None else acc + part
    o_ref[...] = x + _rms(acc, gpost_ref[...])


def _mlp(layer, x, g_pre, w1, w2, g_post):
    n, d = x.shape
    tm = TM_PROJ
    blk = pl.BlockSpec((tm, d), lambda i: (i, 0))
    consts = (g_pre, w1, w2, g_post)
    return pl.pallas_call(
        _mlp_kernel,
        name="mlp",
        grid=(n // tm,),
        in_specs=[blk] + [_layer_spec(c, layer) for c in consts],
        out_specs=blk,
        out_shape=jax.ShapeDtypeStruct(x.shape, F32),
        compiler_params=pltpu.CompilerParams(
            dimension_semantics=("arbitrary",),
            vmem_limit_bytes=VMEM_LIMIT),
    )(x, *consts)


def kernel(x, g_pre_mix, w_in, b_gate, s5_a_re, s5_a_im, s5_log_dt, s5_b_re, s5_b_im, s5_c_re,
           s5_c_im, s5_d, s5_w_glu, conv_w, fox_b_f, w_branch, w_out, g_post_mix, g_pre_mlp,
           w_ff1, w_ff2, g_post_mlp):
    bsz, seq, d = x.shape
    depth = g_pre_mix.shape[0]
    n = bsz * seq

    n_act = 6 * BRANCH_WIDTH
    n_v = n_act + BRANCH_WIDTH
    w_in_b = w_in.astype(BF16)
    wv_t = jnp.swapaxes(w_in_b[:, :, n_act:n_v], 1, 2)
    wf_t = jnp.swapaxes(w_in_b[:, :, n_v:n_v + ATTN_HEADS], 1, 2)
    w_g = w_in_b[:, :, n_v + ATTN_HEADS:]
    w_br, w_o = w_branch.astype(BF16), w_out.astype(BF16)
    w1, w2 = w_ff1.astype(BF16), w_ff2.astype(BF16)
    row = lambda v: v.astype(F32).reshape(depth, 1, -1)
    g_mix, g_post, g_mlp, g_post2, b_g = map(
        row, (g_pre_mix, g_post_mix, g_pre_mlp, g_post_mlp, b_gate))
    b_f = fox_b_f.astype(F32).reshape(depth, ATTN_HEADS, 1)
    cw = conv_w.astype(F32)
    s5_params = _s5_params(s5_a_re, s5_a_im, s5_log_dt, s5_b_re, s5_b_im, s5_c_re, s5_c_im,
                           s5_d, s5_w_glu)

    flat = lambda t: t.reshape(n, t.shape[-1])
    for layer in range(depth):
        u, yb, q, k, vt, kc = _in_proj(layer, x, g_mix, w_in_b, wv_t, wf_t, b_f, cw)
        ya = _s5(layer, u, s5_params)
        yc = _attention(q, k, kc, vt)
        x1 = _merge(layer, flat(x), flat(ya), flat(yb), flat(yc), g_mix, w_g, b_g, w_br, w_o,
                    g_post)
        x = _mlp(layer, x1, g_mlp, w1, w2, g_post2).reshape(bsz, seq, d)
    return x
```

```python
import math

import jax
import jax.numpy as jnp
from jax import lax
from jax.experimental import pallas as pl
from jax.experimental.pallas import tpu as pltpu

F32 = jnp.float32
BF16 = jnp.bfloat16

EPS = 1e-6
MASK_VALUE = -1e30

ATTN_HEADS = 8
HEAD_DIM = 64
CONV_K = 3
BRANCH_WIDTH = 512

LANES = 128
SUBLANES = 8
BF16_SUBLANES = 16
VMEM_LIMIT = 56 * 1024 * 1024

TM_PROJ = 512
S5_RUN = 16
S5_T = 16
S5_CBLK = 64
T_ATTN = 1024
TS_ATTN = 256
LOG2E = math.log2(math.e)
CUM_PIECES = 3


def _rms(x, g):
    return x * lax.rsqrt(jnp.mean(x * x, axis=-1, keepdims=True) + EPS) * g


def _transpose_runs(arrs):
    n = len(arrs)
    assert n * S5_RUN == LANES
    run = lax.broadcasted_iota(jnp.int32, arrs[0].shape, 1) // S5_RUN
    d = n // 2
    while d:
        low = (run & d) == 0
        nxt = list(arrs)
        for i in range(n):
            if not i & d:
                a, b = arrs[i], arrs[i + d]
                nxt[i] = jnp.where(low, a, pltpu.roll(b, S5_RUN * d, 1))
                nxt[i + d] = jnp.where(low, pltpu.roll(a, LANES - S5_RUN * d, 1), b)
        arrs = nxt
        d //= 2
    return arrs


def _cumsum_lanes(x):
    n = x.shape[-1]
    lane = lax.broadcasted_iota(jnp.int32, x.shape, x.ndim - 1)
    shift = 1
    while shift < n:
        x = x + jnp.where(lane >= shift, pltpu.roll(x, shift, x.ndim - 1), 0.0)
        shift *= 2
    return x


def _layer_spec(arr, layer, block=None):
    block = arr.shape[1:] if block is None else block
    return pl.BlockSpec((None,) + tuple(block), lambda *_: (layer,) + (0,) * len(block),
                        pipeline_mode=pl.Buffered(1))


def _inproj_kernel(x_ref, g_ref, w_ref, wvt_ref, wft_ref, bf_ref, cw_ref,
                   ut_ref, yb_ref, q_ref, k_ref, vt_ref, kc_ref,
                   vtail_ref, carry_ref, uscr_ref):
    j = pl.program_id(1)
    tm = x_ref.shape[0]
    w = BRANCH_WIDTH

    @pl.when(j == 0)
    def _():
        vtail_ref[...] = jnp.zeros_like(vtail_ref)
        carry_ref[...] = jnp.zeros_like(carry_ref)

    hn = _rms(x_ref[...], g_ref[...]).astype(BF16)

    def proj(c):
        return jnp.dot(hn, w_ref[:, c * w:(c + 1) * w], preferred_element_type=F32)

    u = proj(0)
    n_lt = w // LANES
    for lt in range(n_lt):
        uscr_ref[lt] = u[:, lt * LANES:(lt + 1) * LANES]
    n_sub = tm // S5_T
    runs_per_tile = LANES // S5_RUN
    for lt in range(n_lt):
        for t8 in range(S5_T // runs_per_tile):
            xs = [uscr_ref[lt, pl.ds(t8 * runs_per_tile + tt, n_sub, stride=S5_T), :]
                  for tt in range(runs_per_tile)]
            for gi, flat in enumerate(_transpose_runs(xs)):
                ut_ref[lt * runs_per_tile + gi, :, t8 * LANES:(t8 + 1) * LANES] = flat.astype(BF16)

    vv = proj(3) * proj(1)
    tail = vtail_ref[...]
    row = lax.broadcasted_iota(jnp.int32, vv.shape, 0)
    v1 = jnp.where(row == 0, tail[7:8, :], pltpu.roll(vv, 1, 0))
    v2 = jnp.where(row == 0, tail[6:7, :],
                   jnp.where(row == 1, tail[7:8, :], pltpu.roll(vv, 2, 0)))
    cw = cw_ref[...]
    conv = v2 * cw[0:1, :] + v1 * cw[1:2, :] + vv * cw[2:3, :]
    yb_ref[...] = (proj(2) * conv).astype(BF16)
    vtail_ref[...] = vv[tm - SUBLANES:tm, :]

    q_ref[...] = (proj(4) * (HEAD_DIM ** -0.5 * LOG2E)).astype(BF16)
    k_ref[...] = proj(5).astype(BF16)
    nt_dims = (((1,), (1,)), ((), ()))
    vt_ref[...] = lax.dot_general(wvt_ref[...], hn, nt_dims,
                                  preferred_element_type=F32).astype(BF16)

    fl = lax.dot_general(wft_ref[...], hn, nt_dims, preferred_element_type=F32)
    z = fl + bf_ref[...]
    log_f = (jnp.minimum(z, 0.0) - jnp.log1p(jnp.exp(-jnp.abs(z)))) * LOG2E
    cum = _cumsum_lanes(log_f) + carry_ref[:, 0:1]
    carry_ref[...] = jnp.broadcast_to(cum[:, tm - 1:tm], carry_ref.shape)
    hi = cum.astype(BF16).astype(F32)
    mid = (cum - hi).astype(BF16).astype(F32)
    lo = cum - hi - mid
    pieces = jnp.concatenate(
        [hi, mid, lo, jnp.zeros((LANES - CUM_PIECES * ATTN_HEADS, tm), F32)], axis=0)
    kc_ref[...] = pieces.T.astype(BF16)


def _in_proj(layer, x, g, w_in, wv_t, wf_t, b_f, conv_w):
    bsz, seq, d = x.shape
    tm = TM_PROJ
    n_act = 6 * BRANCH_WIDTH
    n_groups = BRANCH_WIDTH // S5_RUN
    act =jax.ShapeDtypeStruct((bsz, seq, BRANCH_WIDTH), BF16)
    act_spec = pl.BlockSpec((None, tm, BRANCH_WIDTH), lambda b, j: (b, j, 0))
    return pl.pallas_call(
        _inproj_kernel,
        name="in_proj",
        grid=(bsz, seq // tm),
        in_specs=[
            pl.BlockSpec((None, tm, d), lambda b, j: (b, j, 0)),
            _layer_spec(g, layer),
            _layer_spec(w_in, layer, (d, n_act)),
            _layer_spec(wv_t, layer),
            _layer_spec(wf_t, layer),
            _layer_spec(b_f, layer),
            _layer_spec(conv_w, layer),
        ],
        out_specs=[
            pl.BlockSpec((n_groups, None, tm // S5_T, S5_T * S5_RUN), lambda b, j: (0, b, j, 0))
        ] + [act_spec] * 3 + [
            pl.BlockSpec((None, BRANCH_WIDTH, tm), lambda b, j: (b, 0, j)),
            pl.BlockSpec((None, tm, LANES), lambda b, j: (b, j, 0))],
        out_shape=[
            jax.ShapeDtypeStruct((n_groups, bsz, seq // S5_T, S5_T * S5_RUN), BF16)
        ] + [act] * 3 + [
            jax.ShapeDtypeStruct((bsz, BRANCH_WIDTH, seq), BF16),
            jax.ShapeDtypeStruct((bsz, seq, LANES), BF16)],
        scratch_shapes=[pltpu.VMEM((SUBLANES, BRANCH_WIDTH), F32),
                        pltpu.VMEM((ATTN_HEADS, LANES), F32),
                        pltpu.VMEM((BRANCH_WIDTH // LANES, tm, LANES), F32)],
        compiler_params=pltpu.CompilerParams(
            dimension_semantics=("arbitrary", "arbitrary"),
            vmem_limit_bytes=VMEM_LIMIT),
    )(x, g, w_in, wv_t, wf_t, b_f, conv_w)


def _gelu_tanh(x):
    c = math.sqrt(2.0 / math.pi)
    return 0.5 * x * (1.0 + jnp.tanh(c * (x + 0.044715 * (x * x * x))))


def _s5_kernel(u_ref, m_ref, wre_ref, wim_ref, vre_ref, vim_ref, are_ref, aim_ref, d_ref, glu_ref,
               y_ref, xre_ref, xim_ref, stre_ref, stim_ref, yt_ref):
    j = pl.program_id(1)
    gb, bsz, n_chunks, width = u_ref.shape
    n_pairs = gb // 2
    rows = bsz * n_chunks

    @pl.when(j == 0)
    def _():
        stre_ref[...] = jnp.zeros_like(stre_ref)
        stim_ref[...] = jnp.zeros_like(stim_ref)

    def u_rows(g):
        return u_ref[g].reshape(rows, width)

    for pr in range(n_pairs):
        u_pair = jnp.concatenate([u_rows(2 * pr), u_rows(2 * pr + 1)], axis=1)
        xre_ref[pr] = jnp.dot(u_pair, wre_ref[pr], preferred_element_type=F32)
        xim_ref[pr] = jnp.dot(u_pair, wim_ref[pr], preferred_element_type=F32)

    def step(c, carry):
        new = []
        for pr in range(n_pairs):
            sr, si = carry[pr]
            r = pl.ds(c, bsz, stride=n_chunks)
            xr = xre_ref[pr, r, :]
            xi = xim_ref[pr, r, :]
            xre_ref[pr, r, :] = sr
            xim_ref[pr, r, :] = si
            ar = are_ref[:, pr * LANES:(pr + 1) * LANES]
            ai = aim_ref[:, pr * LANES:(pr + 1) * LANES]
            new.append((ar * sr - ai * si + xr, ar * si + ai * sr + xi))
        return tuple(new)

    state = lax.fori_loop(0, n_chunks, step,
                          tuple((stre_ref[pr], stim_ref[pr]) for pr in range(n_pairs)), unroll=4)
    for pr in range(n_pairs):
        stre_ref[pr], stim_ref[pr] = state[pr]

    for g in range(gb):
        u = u_rows(g)
        y = jnp.dot(u, m_ref[g], preferred_element_type=F32)
        y = y + jnp.dot(xre_ref[g // 2].astype(BF16), vre_ref[g], preferred_element_type=F32)
        y = y + jnp.dot(xim_ref[g // 2].astype(BF16), vim_ref[g], preferred_element_type=F32)
        y = _gelu_tanh(y + d_ref[g] * u.astype(F32))
        gate = jnp.dot(y.astype(BF16), glu_ref[g], preferred_element_type=F32)
        y = y * jax.nn.sigmoid(gate)
        for t8 in range(width // LANES):
            yt_ref[g, t8] = y[:, t8 * LANES:(t8 + 1) * LANES]

    for b in range(bsz):
        for t8 in range(width // LANES):
            zs = [yt_ref[g, t8, b * n_chunks:(b + 1) * n_chunks, :] for g in range(gb)]
            for tt, rows_t in enumerate(_transpose_runs(zs)):
                y_ref[b, pl.ds(t8 * gb + tt, n_chunks, stride=S5_T), :] = rows_t


def _s5_params(a_re, a_im, log_dt, b_re, b_im, c_re, c_im, d_skip, w_glu):
    depth, g, p, h = b_re.shape
    t_sub = S5_T
    f = lambda v: v.astype(F32)
    lam_re = jnp.minimum(f(a_re), -1e-4)
    lam_im = f(a_im)
    dt = jnp.exp(f(log_dt))[..., None]
    rho, theta = lam_re * dt, lam_im * dt

    abar_re = jnp.exp(rho) * jnp.cos(theta)
    abar_im = jnp.exp(rho) * jnp.sin(theta)
    inv = 1.0 / (lam_re * lam_re + lam_im * lam_im)
    coef_re = (((abar_re - 1.0) * lam_re + abar_im * lam_im) * inv)[..., None]
    coef_im = ((abar_im * lam_re - (abar_re - 1.0) * lam_im) * inv)[..., None]
    bbar_re = coef_re * f(b_re) - coef_im * f(b_im)
    bbar_im = coef_re * f(b_im) + coef_im * f(b_re)

    tau = jnp.arange(t_sub + 1, dtype=F32)[:, None]
    pw_mag = jnp.exp(tau * rho[:, :, None, :])
    pw_re = pw_mag * jnp.cos(tau * theta[:, :, None, :])
    pw_im = pw_mag * jnp.sin(tau * theta[:, :, None, :])

    pr, pi = pw_re[:, :, :t_sub, :, None], pw_im[:, :, :t_sub, :, None]
    ab_re = pr * bbar_re[:, :, None] - pi * bbar_im[:, :, None]
    ab_im = pr * bbar_im[:, :, None] + pi * bbar_re[:, :, None]

    kern = (jnp.einsum('dgop,dgtph->dgtoh', f(c_re), ab_re)
            - jnp.einsum('dgop,dgtph->dgtoh', f(c_im), ab_im))
    lag = jnp.arange(t_sub)[None, :] - jnp.arange(t_sub)[:, None]
    m = jnp.where((lag >= 0)[None, None, :, :, None, None],
                  kern[:, :, jnp.clip(lag, 0, t_sub - 1)], 0.0)
    m = m.transpose(0, 1, 2, 5, 3, 4).reshape(depth, g, t_sub * h, t_sub * h)

    def pair_block_diag(w):
        w = w[:, :, ::-1].transpose(0, 1, 2, 4, 3).reshape(depth, g // 2, 2, t_sub * h, p)
        return jnp.einsum('dqirc,ij->dqirjc', w, jnp.eye(2, dtype=F32)).reshape(
            depth, g // 2, 2 * t_sub * h, 2 * p)

    w_re, w_im = pair_block_diag(ab_re), pair_block_diag(ab_im)

    p1_re, p1_im = pw_re[:, :, 1:, None, :], pw_im[:, :, 1:, None, :]
    ca_re = f(c_re)[:, :, None] * p1_re - f(c_im)[:, :, None] * p1_im
    ca_im = f(c_re)[:, :, None] * p1_im + f(c_im)[:, :, None] * p1_re

    def pair_rows(v):
        v = v.transpose(0, 1, 4, 2, 3).reshape(depth, g // 2, 2, p, t_sub * h)
        return jnp.einsum('dqipc,ij->dqijpc', v, jnp.eye(2, dtype=F32)).reshape(
            depth, g, 2 * p, t_sub * h)

    v_re, v_im = pair_rows(ca_re), pair_rows(-ca_im)

    at_re = jnp.broadcast_to(pw_re[:, :, t_sub].reshape(depth, 1, g * p), (depth, SUBLANES, g * p))
    at_im = jnp.broadcast_to(pw_im[:, :, t_sub].reshape(depth, 1, g * p), (depth, SUBLANES, g * p))

    d_t = jnp.tile(f(d_skip)[:, :, None, :], (1, 1, t_sub, 1)).reshape(depth, g, 1, t_sub * h)
    glu = jnp.einsum('dghk,ij->dgihjk', f(w_glu), jnp.eye(t_sub, dtype=F32)).reshape(
        depth, g, t_sub * h, t_sub * h)
    bf = lambda v: v.astype(BF16)
    return bf(m), bf(w_re), bf(w_im), bf(v_re), bf(v_im), at_re, at_im, d_t, bf(glu)


def _s5(layer, u_t, params):
    g, bsz, n_chunks, width = u_t.shape
    m, w_re, w_im, v_re, v_im, at_re, at_im, d_t, glu = params
    gb = LANES // S5_RUN
    cb = S5_CBLK
    rows = bsz * cb
    n_pairs = gb // 2
    assert bsz == SUBLANES and width == 2 * LANES and S5_T == 2 * gb

    def per_group(arr, groups):
        blk = (None, groups) + arr.shape[2:]
        return pl.BlockSpec(blk, lambda i, j: (layer, i) + (0,) * (arr.ndim - 2))

    decay_spec = pl.BlockSpec((None, SUBLANES, n_pairs * LANES), lambda i, j: (layer, 0, i))
    return pl.pallas_call(
        _s5_kernel,
        name="s5",
        grid=(g // gb, n_chunks // cb),
        in_specs=[pl.BlockSpec((gb, bsz, cb, width), lambda i, j: (i, 0, j, 0)),
                  per_group(m, gb), per_group(w_re, n_pairs), per_group(w_im, n_pairs),
                  per_group(v_re, gb), per_group(v_im, gb), decay_spec, decay_spec,
                  per_group(d_t, gb), per_group(glu, gb)],
        out_specs=pl.BlockSpec((bsz, cb * S5_T, LANES), lambda i, j: (0, j, i)),
        out_shape=jax.ShapeDtypeStruct((bsz, n_chunks * S5_T, g * S5_RUN), F32),
        scratch_shapes=[
            pltpu.VMEM((n_pairs, rows, LANES), F32),
            pltpu.VMEM((n_pairs, rows, LANES), F32),
            pltpu.VMEM((n_pairs, SUBLANES, LANES), F32),
            pltpu.VMEM((n_pairs, SUBLANES, LANES), F32),
            pltpu.VMEM((gb, width // LANES, rows, LANES), F32),
        ],
        compiler_params=pltpu.CompilerParams(
            dimension_semantics=("arbitrary", "arbitrary"),
            vmem_limit_bytes=VMEM_LIMIT),
    )(u_t, m, w_re, w_im, v_re, v_im, at_re, at_im, d_t, glu)


def _attn_kernel(q_ref, k_ref, kc_ref, vt_ref, o_ref, s_ref, bmax_ref, m_ref, acc_ref):
    pair = pl.program_id(1)
    qi = pl.program_id(2)
    tq = q_ref.shape[0]
    ts = TS_ATTN
    n_strips = tq // ts
    assert n_strips % 2 == 0
    q = q_ref[...]
    lane = lax.broadcasted_iota(jnp.int32, (ts, LANES), 1)
    k_pos = lax.broadcasted_iota(jnp.int32, (ts, ts), 0)
    q_pos = lax.broadcasted_iota(jnp.int32, (ts, ts), 1)
    nt_dims = (((1,), (1,)), ((), ()))

    chains = [(st, hh) for st in range(n_strips) for hh in range(2)]
    qs = {}
    for st, hh in chains:
        q_st = q[st * ts:(st + 1) * ts, :]
        head_lanes = (lane >= HEAD_DIM) if hh else (lane < HEAD_DIM)
        piece_lanes = (lane < CUM_PIECES * ATTN_HEADS) & (lane % ATTN_HEADS == 2 * pair + hh)
        qs[st, hh] = jnp.concatenate(
            [jnp.where(head_lanes, q_st, jnp.zeros_like(q_st)),
             jnp.where(piece_lanes, -1.0, 0.0).astype(BF16)], axis=1)

    def key_start(blk):
        return pl.multiple_of(blk * ts, ts)

    def scores(blk, buf, first_strip=0):
        k0 = key_start(blk)
        kb = jnp.concatenate([k_ref[pl.ds(k0, ts), :], kc_ref[pl.ds(k0, ts), :]], axis=1)
        for c, (st, hh) in enumerate(chains):
            if st < first_strip:
                continue
            s = lax.dot_general(kb, qs[st, hh], nt_dims, preferred_element_type=F32)
            s_ref[buf, c] = s
            bmax_ref[buf, c] = jnp.max(s, axis=0, keepdims=True)

    def softmax_pv(blk, buf, first_strip=0, diag_strip=None):
        vt = vt_ref[:, pl.ds(key_start(blk), ts)]
        vt_ones = [jnp.concatenate([vt[hh * HEAD_DIM:(hh + 1) * HEAD_DIM, :], ones_rows], axis=0)
                   for hh in range(2)]
        for c, (st, hh) in enumerate(chains):
            if st < first_strip:
                continue
            s = s_ref[buf, c]
            blk_max = bmax_ref[buf, c]
            if st == diag_strip:
                s = jnp.where(k_pos <= q_pos, s, MASK_VALUE)
                blk_max = jnp.max(s, axis=0, keepdims=True)
            m = m_ref[c]
            m_new = jnp.maximum(m, blk_max)
            alpha = jnp.exp2(m - m_new)
            p = jnp.exp2(s - m_new).astype(BF16)
            pv = jnp.dot(vt_ones[hh], p, preferred_element_type=F32)
            acc_ref[c] = alpha * acc_ref[c] + pv
            m_ref[c] = m_new

    m_ref[...] = jnp.full(m_ref.shape, MASK_VALUE, F32)
    acc_ref[...] = jnp.zeros_like(acc_ref)
    ones_rows = jnp.ones((BF16_SUBLANES, ts), BF16)

    scores(0, 0)

    def step(t, _):
        scores(2 * t + 1, 1)
        softmax_pv(2 * t, 0)
        scores(2 * t + 2, 0)
        softmax_pv(2 * t + 1, 1)
        return 0

    n_full = qi * n_strips
    lax.fori_loop(0, qi * (n_strips // 2), step, 0)
    for d in range(n_strips):
        if d + 1 < n_strips:
            scores(n_full + d + 1, (d + 1) % 2, first_strip=d + 1)
        softmax_pv(n_full + d, d % 2, first_strip=d, diag_strip=d)

    out_t = jnp.concatenate(
        [jnp.concatenate([acc_ref[c, 0:HEAD_DIM, :] / acc_ref[c, HEAD_DIM:HEAD_DIM + 1, :]
                          for c in range(st * 2, st * 2 + 2)], axis=0)
         for st in range(n_strips)], axis=1)
    o_ref[...] = out_t.T.astype(BF16)


def _attention(q, k, kc, vt):
    bsz, seq, w = q.shape
    pairs = w // LANES
    tq = T_ATTN
    ts = TS_ATTN
    n_chains = 2 * (tq // ts)
    return pl.pallas_call(
        _attn_kernel,
        name="fox_attn",
        grid=(bsz, pairs, seq // tq),
        in_specs=[
            pl.BlockSpec((None, tq, LANES), lambda b, p, i: (b, i, p)),
            pl.BlockSpec((None, seq, LANES), lambda b, p, i: (b, 0, p)),
            pl.BlockSpec((None, seq, LANES), lambda b, p, i: (b, 0, 0)),
            pl.BlockSpec((None, LANES, seq), lambda b, p, i: (b, p, 0)),
        ],
        out_specs=pl.BlockSpec((None, tq, LANES), lambda b, p, i: (b, i, p)),
        out_shape=jax.ShapeDtypeStruct(q.shape, BF16),
        scratch_shapes=[
            pltpu.VMEM((2, n_chains, ts, ts), F32),
            pltpu.VMEM((2, n_chains, 1, ts), F32),
            pltpu.VMEM((n_chains, 1, ts), F32),
            pltpu.VMEM((n_chains, HEAD_DIM + BF16_SUBLANES, ts), F32),
        ],
        compiler_params=pltpu.CompilerParams(
            dimension_semantics=("arbitrary", "arbitrary", "arbitrary"),
            vmem_limit_bytes=VMEM_LIMIT),
    )(q, k, kc, vt)


def _merge_kernel(x_ref, ya_ref, yb_ref, yc_ref, gpre_ref, wg_ref, bg_ref, wbr_ref, wout_ref,
                  gpost_ref, o_ref):
    x = x_ref[...]
    d = x.shape[1]
    hn = _rms(x, gpre_ref[...]).astype(BF16)
    merged = None
    for i, y_ref in enumerate((ya_ref, yb_ref, yc_ref)):
        cols = slice(i * d, (i + 1) * d)
        gate = jax.nn.sigmoid(
            jnp.dot(hn, wg_ref[:, cols], preferred_element_type=F32) + bg_ref[:, cols])
        br = jnp.dot(y_ref[...].astype(BF16), wbr_ref[i * BRANCH_WIDTH:(i + 1) * BRANCH_WIDTH, :],
                     preferred_element_type=F32)
        merged = gate * br if merged is None else merged + gate * br
    mix = jnp.dot(merged.astype(BF16), wout_ref[...], preferred_element_type=F32)
    o_ref[...] = x + _rms(mix, gpost_ref[...])


def _merge(layer, x, ya, yb, yc, g_pre, w_g, b_g, w_br, w_out, g_post):
    n, d = x.shape
    tm = TM_PROJ
    row_blk = lambda width: pl.BlockSpec((tm, width), lambda i: (i, 0))
    consts = (g_pre, w_g, b_g, w_br, w_out, g_post)
    return pl.pallas_call(
        _merge_kernel,
        name="merge",
        grid=(n // tm,),
        in_specs=[row_blk(d)] + [row_blk(BRANCH_WIDTH)] * 3 + [_layer_spec(c, layer) for c in consts],
        out_specs=row_blk(d),
        out_shape=jax.ShapeDtypeStruct(x.shape, F32),
        compiler_params=pltpu.CompilerParams(
            dimension_semantics=("arbitrary",),
            vmem_limit_bytes=VMEM_LIMIT),
    )(x, ya, yb, yc, *consts)


def _mlp_kernel(x_ref, gpre_ref, w1_ref, w2_ref, gpost_ref, o_ref):
    x = x_ref[...]
    d = x.shape[1]
    hn = _rms(x, gpre_ref[...]).astype(BF16)
    acc = None
    for c in range(w1_ref.shape[1] // d):
        cols = slice(c * d, (c + 1) * d)
        h1 = jnp.dot(hn, w1_ref[:, cols], preferred_element_type=F32)
        h1 = jnp.square(jnp.maximum(h1, 0.0)).astype(BF16)
        part = jnp.dot(h1, w2_ref[cols, :], preferred_element_type=F32)
        acc = part if acc is None else acc + part
    o_ref[...] = x + _rms(acc, gpost_ref[...])


def _mlp(layer, x, g_pre, w1, w2, g_post):
    n, d = x.shape
    tm = TM_PROJ
    blk = pl.BlockSpec((tm, d), lambda i: (i, 0))
    consts = (g_pre, w1, w2, g_post)
    return pl.pallas_call(
        _mlp_kernel,
        name="mlp",
        grid=(n // tm,),
        in_specs=[blk] + [_layer_spec(c, layer) for c in consts],
        out_specs=blk,
        out_shape=jax.ShapeDtypeStruct(x.shape, F32),
        compiler_params=pltpu.CompilerParams(
            dimension_semantics=("arbitrary",),
            vmem_limit_bytes=VMEM_LIMIT),
    )(x, *consts)


def kernel(x, g_pre_mix, w_in, b_gate, s5_a_re, s5_a_im, s5_log_dt, s5_b_re, s5_b_im, s5_c_re,
           s5_c_im, s5_d, s5_w_glu, conv_w, fox_b_f, w_branch, w_out, g_post_mix, g_pre_mlp,
           w_ff1, w_ff2, g_post_mlp):
    bsz, seq, d = x.shape
    depth = g_pre_mix.shape[0]
    n = bsz * seq

    n_act = 6 * BRANCH_WIDTH
    n_v = n_act + BRANCH_WIDTH
    w_in_b = w_in.astype(BF16)
    wv_t = jnp.swapaxes(w_in_b[:, :, n_act:n_v], 1, 2)
    wf_t = jnp.swapaxes(w_in_b[:, :, n_v:n_v + ATTN_HEADS], 1, 2)
    w_g = w_in_b[:, :, n_v + ATTN_HEADS:]
    w_br, w_o = w_branch.astype(BF16), w_out.astype(BF16)
    w1, w2 = w_ff1.astype(BF16), w_ff2.astype(BF16)
    row = lambda v: v.astype(F32).reshape(depth, 1, -1)
    g_mix, g_post, g_mlp, g_post2, b_g = map(
        row, (g_pre_mix, g_post_mix, g_pre_mlp, g_post_mlp, b_gate))
    b_f = fox_b_f.astype(F32).reshape(depth, ATTN_HEADS, 1)
    cw = conv_w.astype(F32)
    s5_params = _s5_params(s5_a_re, s5_a_im, s5_log_dt, s5_b_re, s5_b_im, s5_c_re, s5_c_im,
                           s5_d, s5_w_glu)

    flat = lambda t: t.reshape(n, t.shape[-1])
    for layer in range(depth):
        ut, yb, q, k, vt, kc = _in_proj(layer, x, g_mix, w_in_b, wv_t, wf_t, b_f, cw)
        ya = _s5(layer, ut, s5_params)
        yc = _attention(q, k, kc, vt)
        x1 = _merge(layer, flat(x), flat(ya), flat(yb), flat(yc), g_mix, w_g, b_g, w_br, w_o,
                    g_post)
        x = _mlp(layer, x1, g_mlp, w1, w2, g_post2).reshape(bsz, seq, d)
    return x
```

```python
import math

import jax
import jax.numpy as jnp
from jax import lax
from jax.experimental import pallas as pl
from jax.experimental.pallas import tpu as pltpu

F32 = jnp.float32
BF16 = jnp.bfloat16

EPS = 1e-6
MASK_VALUE = -1e30

ATTN_HEADS = 8
HEAD_DIM = 64
CONV_K = 3
BRANCH_WIDTH = 512

LANES = 128
SUBLANES = 8
BF16_SUBLANES = 16
VMEM_LIMIT = 56 * 1024 * 1024

TM_PROJ = 1024
S5_RUN = 16
S5_T = 16
S5_CBLK = 64
T_ATTN = 1024
TS_ATTN = 256
LOG2E = math.log2(math.e)
CUM_PIECES = 3


def _rms(x, g):
    return x * lax.rsqrt(jnp.mean(x * x, axis=-1, keepdims=True) + EPS) * g


def _transpose_runs(arrs):
    n = len(arrs)
    assert n * S5_RUN == LANES
    run = lax.broadcasted_iota(jnp.int32, arrs[0].shape, 1) // S5_RUN
    d = n // 2
    while d:
        low = (run & d) == 0
        nxt = list(arrs)
        for i in range(n):
            if not i & d:
                a, b = arrs[i], arrs[i + d]
                nxt[i] = jnp.where(low, a, pltpu.roll(b, S5_RUN * d, 1))
                nxt[i + d] = jnp.where(low, pltpu.roll(a, LANES - S5_RUN * d, 1), b)
        arrs = nxt
        d //= 2
    return arrs


def _cumsum_lanes(x):
    n = x.shape[-1]
    lane = lax.broadcasted_iota(jnp.int32, x.shape, x.ndim - 1)
    shift = 1
    while shift < n:
        x = x + jnp.where(lane >= shift, pltpu.roll(x, shift, x.ndim - 1), 0.0)
        shift *= 2
    return x


def _layer_spec(arr, layer, block=None):
    block = arr.shape[1:] if block is None else block
    return pl.BlockSpec((None,) + tuple(block), lambda *_: (layer,) + (0,) * len(block),
                        pipeline_mode=pl.Buffered(1))


def _inproj_kernel(x_ref, g_ref, w_ref, wvt_ref, wft_ref, bf_ref, cw_ref,
                   ut_ref, yb_ref, q_ref, k_ref, vt_ref, kc_ref,
                   vtail_ref, carry_ref, uscr_ref):
    j = pl.program_id(1)
    tm = x_ref.shape[0]
    w = BRANCH_WIDTH

    @pl.when(j == 0)
    def _():
        vtail_ref[...] = jnp.zeros_like(vtail_ref)
        carry_ref[...] = jnp.zeros_like(carry_ref)

    hn = _rms(x_ref[...], g_ref[...]).astype(BF16)

    def proj(c):
        return jnp.dot(hn, w_ref[:, c * w:(c + 1) * w], preferred_element_type=F32)

    u = proj(0)
    n_lt = w // LANES
    for lt in range(n_lt):
        uscr_ref[lt] = u[:, lt * LANES:(lt + 1) * LANES]
    n_sub = tm // S5_T
    runs_per_tile = LANES // S5_RUN
    for lt in range(n_lt):
        for t8 in range(S5_T // runs_per_tile):
            xs = [uscr_ref[lt, pl.ds(t8 * runs_per_tile + tt, n_sub, stride=S5_T), :]
                  for tt in range(runs_per_tile)]
            for gi, flat in enumerate(_transpose_runs(xs)):
                ut_ref[lt * runs_per_tile + gi, :, t8 * LANES:(t8 + 1) * LANES] = flat.astype(BF16)

    vv = proj(3) * proj(1)
    tail = vtail_ref[...]
    row = lax.broadcasted_iota(jnp.int32, vv.shape, 0)
    v1 = jnp.where(row == 0, tail[7:8, :], pltpu.roll(vv, 1, 0))
    v2 = jnp.where(row == 0, tail[6:7, :],
                   jnp.where(row == 1, tail[7:8, :], pltpu.roll(vv, 2, 0)))
    cw = cw_ref[...]
    conv = v2 * cw[0:1, :] + v1 * cw[1:2, :] + vv * cw[2:3, :]
    yb_ref[...] = (proj(2) * conv).astype(BF16)
    vtail_ref[...] = vv[tm - SUBLANES:tm, :]

    q_ref[...] = (proj(4) * (HEAD_DIM ** -0.5 * LOG2E)).astype(BF16)
    k_ref[...] = proj(5).astype(BF16)
    nt_dims = (((1,), (1,)), ((), ()))
    vt_ref[...] = lax.dot_general(wvt_ref[...], hn, nt_dims,
                                  preferred_element_type=F32).astype(BF16)

    fl = lax.dot_general(wft_ref[...], hn, nt_dims, preferred_element_type=F32)
    z = fl + bf_ref[...]
    log_f = (jnp.minimum(z, 0.0) - jnp.log1p(jnp.exp(-jnp.abs(z)))) * LOG2E
    cum = _cumsum_lanes(log_f) + carry_ref[:, 0:1]
    carry_ref[...] = jnp.broadcast_to(cum[:, tm - 1:tm], carry_ref.shape)
    hi = cum.astype(BF16).astype(F32)
    mid = (cum - hi).astype(BF16).astype(F32)
    lo = cum - hi - mid
    pieces = jnp.concatenate(
        [hi, mid, lo, jnp.zeros((LANES - CUM_PIECES * ATTN_HEADS, tm), F32)], axis=0)
    kc_ref[...] = pieces.T.astype(BF16)


def _in_proj(layer, x, g, w_in, wv_t, wf_t, b_f, conv_w):
    bsz, seq, d = x.shape
    tm = TM_PROJ
    n_act = 6 * BRANCH_WIDTH
    n_groups = BRANCH_WIDTH // S5_RUN
    act =jax.ShapeDtypeStruct((bsz, seq, BRANCH_WIDTH), BF16)
    act_spec = pl.BlockSpec((None, tm, BRANCH_WIDTH), lambda b, j: (b, j, 0))
    return pl.pallas_call(
        _inproj_kernel,
        name="in_proj",
        grid=(bsz, seq // tm),
        in_specs=[
            pl.BlockSpec((None, tm, d), lambda b, j: (b, j, 0)),
            _layer_spec(g, layer),
            _layer_spec(w_in, layer, (d, n_act)),
            _layer_spec(wv_t, layer),
            _layer_spec(wf_t, layer),
            _layer_spec(b_f, layer),
            _layer_spec(conv_w, layer),
        ],
        out_specs=[
            pl.BlockSpec((n_groups, None, tm // S5_T, S5_T * S5_RUN), lambda b, j: (0, b, j, 0))
        ] + [act_spec] * 3 + [
            pl.BlockSpec((None, BRANCH_WIDTH, tm), lambda b, j: (b, 0, j)),
            pl.BlockSpec((None, tm, LANES), lambda b, j: (b, j, 0))],
        out_shape=[
            jax.ShapeDtypeStruct((n_groups, bsz, seq // S5_T, S5_T * S5_RUN), BF16)
        ] + [act] * 3 + [
            jax.ShapeDtypeStruct((bsz, BRANCH_WIDTH, seq), BF16),
            jax.ShapeDtypeStruct((bsz, seq, LANES), BF16)],
        scratch_shapes=[pltpu.VMEM((SUBLANES, BRANCH_WIDTH), F32),
                        pltpu.VMEM((ATTN_HEADS, LANES), F32),
                        pltpu.VMEM((BRANCH_WIDTH // LANES, tm, LANES), F32)],
        compiler_params=pltpu.CompilerParams(
            dimension_semantics=("arbitrary", "arbitrary"),
            vmem_limit_bytes=VMEM_LIMIT),
    )(x, g, w_in, wv_t, wf_t, b_f, conv_w)


def _gelu_tanh(x):
    c = math.sqrt(2.0 / math.pi)
    return 0.5 * x * (1.0 + jnp.tanh(c * (x + 0.044715 * (x * x * x))))


def _s5_kernel(u_ref, kern_ref, wre_ref, wim_ref, vre_ref, vim_ref, are_ref, aim_ref, d_ref,
               glu0_ref, y_ref, xre_ref, xim_ref, stre_ref, stim_ref, yt_ref, m_ref, glu_ref):
    j = pl.program_id(1)
    gb, bsz, n_chunks, width = u_ref.shape
    n_pairs = gb // 2
    rows = bsz * n_chunks

    @pl.when(j == 0)
    def _():
        stre_ref[...] = jnp.zeros_like(stre_ref)
        stim_ref[...] = jnp.zeros_like(stim_ref)
        lane = lax.broadcasted_iota(jnp.int32, (S5_RUN, width), 1)
        for g in range(gb):
            k_slab = kern_ref[g]
            w_slab = glu0_ref[g]
            for jj in range(S5_T):
                rows_j = slice(jj * S5_RUN, (jj + 1) * S5_RUN)
                k_j = pltpu.roll(k_slab, jj * S5_RUN, 1) if jj else k_slab
                w_j = pltpu.roll(w_slab, jj * S5_RUN, 1) if jj else w_slab
                m_ref[g, rows_j, :] = jnp.where(lane >= jj * S5_RUN, k_j, 0.0).astype(BF16)
                glu_ref[g, rows_j, :] = w_j.astype(BF16)

    def u_rows(g):
        return u_ref[g].reshape(rows, width)

    for pr in range(n_pairs):
        u_pair = jnp.concatenate([u_rows(2 * pr), u_rows(2 * pr + 1)], axis=1)
        xre_ref[pr] = jnp.dot(u_pair, wre_ref[pr], preferred_element_type=F32)
        xim_ref[pr] = jnp.dot(u_pair, wim_ref[pr], preferred_element_type=F32)

    def step(c, carry):
        new = []
        for pr in range(n_pairs):
            sr, si = carry[pr]
            r = pl.ds(c, bsz, stride=n_chunks)
            xr = xre_ref[pr, r, :]
            xi = xim_ref[pr, r, :]
            xre_ref[pr, r, :] = sr
            xim_ref[pr, r, :] = si
            ar = are_ref[:, pr * LANES:(pr + 1) * LANES]
            ai = aim_ref[:, pr * LANES:(pr + 1) * LANES]
            new.append((ar * sr - ai * si + xr, ar * si + ai * sr + xi))
        return tuple(new)

    state = lax.fori_loop(0, n_chunks, step,
                          tuple((stre_ref[pr], stim_ref[pr]) for pr in range(n_pairs)), unroll=4)
    for pr in range(n_pairs):
        stre_ref[pr], stim_ref[pr] = state[pr]

    for g in range(gb):
        u = u_rows(g)
        y = jnp.dot(u, m_ref[g], preferred_element_type=F32)
        y = y + jnp.dot(xre_ref[g // 2].astype(BF16), vre_ref[g], preferred_element_type=F32)
        y = y + jnp.dot(xim_ref[g // 2].astype(BF16), vim_ref[g], preferred_element_type=F32)
        y = _gelu_tanh(y + d_ref[g] * u.astype(F32))
        gate = jnp.dot(y.astype(BF16), glu_ref[g], preferred_element_type=F32)
        y = y * jax.nn.sigmoid(gate)
        for t8 in range(width // LANES):
            yt_ref[g, t8] = y[:, t8 * LANES:(t8 + 1) * LANES]

    for b in range(bsz):
        for t8 in range(width // LANES):
            zs = [yt_ref[g, t8, b * n_chunks:(b + 1) * n_chunks, :] for g in range(gb)]
            for tt, rows_t in enumerate(_transpose_runs(zs)):
                y_ref[b, pl.ds(t8 * gb + tt, n_chunks, stride=S5_T), :] = rows_t


def _s5_params(a_re, a_im, log_dt, b_re, b_im, c_re, c_im, d_skip, w_glu):
    depth, g, p, h = b_re.shape
    t_sub = S5_T
    f = lambda v: v.astype(F32)
    tr = lambda v: jnp.swapaxes(f(v), -1, -2)
    lam_re = jnp.minimum(f(a_re), -1e-4)
    lam_im = f(a_im)
    dt = jnp.exp(f(log_dt))[..., None]
    rho, theta = lam_re * dt, lam_im * dt

    abar_re = jnp.exp(rho) * jnp.cos(theta)
    abar_im = jnp.exp(rho) * jnp.sin(theta)
    inv = 1.0 / (lam_re * lam_re + lam_im * lam_im)
    coef_re = (((abar_re - 1.0) * lam_re + abar_im * lam_im) * inv)[:, :, None, :]
    coef_im = ((abar_im * lam_re - (abar_re - 1.0) * lam_im) * inv)[:, :, None, :]
    bbar_re = coef_re * tr(b_re) - coef_im * tr(b_im)
    bbar_im = coef_re * tr(b_im) + coef_im * tr(b_re)

    tau = jnp.arange(t_sub + 1, dtype=F32)[:, None]
    pw_mag = jnp.exp(tau * rho[:, :, None, :])
    pw_re = pw_mag * jnp.cos(tau * theta[:, :, None, :])
    pw_im = pw_mag * jnp.sin(tau * theta[:, :, None, :])

    pr, pi = pw_re[:, :, :t_sub, None, :], pw_im[:, :, :t_sub, None, :]
    ab_re = pr * bbar_re[:, :, None] - pi * bbar_im[:, :, None]
    ab_im = pr * bbar_im[:, :, None] + pi * bbar_re[:, :, None]

    kern = (jnp.einsum('dgop,dgthp->dghto', f(c_re), ab_re)
            - jnp.einsum('dgop,dgthp->dghto', f(c_im), ab_im)).reshape(depth, g, h, t_sub * h)

    def pair_block_diag(w):
        w = jnp.flip(w, axis=2).reshape(depth, g // 2, 2, t_sub * h, p)
        return jnp.einsum('dqirc,ij->dqirjc', w, jnp.eye(2, dtype=F32)).reshape(
            depth, g // 2, 2 * t_sub * h, 2 * p).astype(BF16)

    w_re, w_im = pair_block_diag(ab_re), pair_block_diag(ab_im)

    p1_re, p1_im = tr(pw_re[:, :, 1:])[..., None], tr(pw_im[:, :, 1:])[..., None]
    ct_re, ct_im = tr(c_re)[:, :, :, None, :], tr(c_im)[:, :, :, None, :]

    def pair_rows(v):
        v = v.reshape(depth, g // 2, 2, p, t_sub * h)
        return jnp.einsum('dqipc,ij->dqijpc', v, jnp.eye(2, dtype=F32)).reshape(
            depth, g, 2 * p, t_sub * h).astype(BF16)

    v_re = pair_rows(ct_re * p1_re - ct_im * p1_im)
    v_im = pair_rows(-(ct_re * p1_im + ct_im * p1_re))

    at_re = jnp.broadcast_to(pw_re[:, :, t_sub].reshape(depth, 1, g * p), (depth, SUBLANES, g * p))
    at_im = jnp.broadcast_to(pw_im[:, :, t_sub].reshape(depth, 1, g * p), (depth, SUBLANES, g * p))

    d_t = jnp.tile(f(d_skip)[:, :, None, :], (1, 1, t_sub, 1)).reshape(depth, g, 1, t_sub * h)
    glu0 = jnp.pad(f(w_glu), ((0, 0), (0, 0), (0, 0), (0, (t_sub - 1) * h)))
    return kern, w_re, w_im, v_re, v_im, at_re, at_im, d_t, glu0


def _s5(layer, u_t, params):
    g, bsz, n_chunks, width = u_t.shape
    kern, w_re, w_im, v_re, v_im, at_re, at_im, d_t, glu0 = params
    gb = LANES // S5_RUN
    cb = S5_CBLK
    rows = bsz * cb
    n_pairs = gb // 2
    assert bsz == SUBLANES and width == 2 * LANES and S5_T == 2 * gb

    def per_group(arr, groups):
        blk = (None, groups) + arr.shape[2:]
        return pl.BlockSpec(blk, lambda i, j: (layer, i) + (0,) * (arr.ndim - 2))

    decay_spec = pl.BlockSpec((None, SUBLANES, n_pairs * LANES), lambda i, j: (layer, 0, i))
    return pl.pallas_call(
        _s5_kernel,
        name="s5",
        grid=(g // gb, n_chunks // cb),
        in_specs=[pl.BlockSpec((gb, bsz, cb, width), lambda i, j: (i, 0, j, 0)),
                  per_group(kern, gb), per_group(w_re, n_pairs), per_group(w_im, n_pairs),
                  per_group(v_re, gb), per_group(v_im, gb), decay_spec, decay_spec,
                  per_group(d_t, gb), per_group(glu0, gb)],
        out_specs=pl.BlockSpec((bsz, cb * S5_T, LANES), lambda i, j: (0, j, i)),
        out_shape=jax.ShapeDtypeStruct((bsz, n_chunks * S5_T, g * S5_RUN), F32),
        scratch_shapes=[
            pltpu.VMEM((n_pairs, rows, LANES), F32),
            pltpu.VMEM((n_pairs, rows, LANES), F32),
            pltpu.VMEM((n_pairs, SUBLANES, LANES), F32),
            pltpu.VMEM((n_pairs, SUBLANES, LANES), F32),
            pltpu.VMEM((gb, width // LANES, rows, LANES), F32),
            pltpu.VMEM((gb, width, width), BF16),
            pltpu.VMEM((gb, width, width), BF16),
        ],
        compiler_params=pltpu.CompilerParams(
            dimension_semantics=("arbitrary", "arbitrary"),
            vmem_limit_bytes=VMEM_LIMIT),
    )(u_t, kern, w_re, w_im, v_re, v_im, at_re, at_im, d_t, glu0)


def _attn_kernel(q_ref, k_ref, kc_ref, vt_ref, o_ref, s_ref, bmax_ref, m_ref, acc_ref):
    pair = pl.program_id(1)
    qi = pl.program_id(2)
    tq = q_ref.shape[0]
    ts = TS_ATTN
    n_strips = tq // ts
    assert n_strips % 2 == 0
    q = q_ref[...]
    lane = lax.broadcasted_iota(jnp.int32, (ts, LANES), 1)
    k_pos = lax.broadcasted_iota(jnp.int32, (ts, ts), 0)
    q_pos = lax.broadcasted_iota(jnp.int32, (ts, ts), 1)
    nt_dims = (((1,), (1,)), ((), ()))

    chains = [(st, hh) for st in range(n_strips) for hh in range(2)]
    qs = {}
    for st, hh in chains:
        q_st = q[st * ts:(st + 1) * ts, :]
        head_lanes = (lane >= HEAD_DIM) if hh else (lane < HEAD_DIM)
        piece_lanes = (lane < CUM_PIECES * ATTN_HEADS) & (lane % ATTN_HEADS == 2 * pair + hh)
        qs[st, hh] = jnp.concatenate(
            [jnp.where(head_lanes, q_st, jnp.zeros_like(q_st)),
             jnp.where(piece_lanes, -1.0, 0.0).astype(BF16)], axis=1)

    def key_start(blk):
        return pl.multiple_of(blk * ts, ts)

    def scores(blk, buf, first_strip=0):
        k0 = key_start(blk)
        kb = jnp.concatenate([k_ref[pl.ds(k0, ts), :], kc_ref[pl.ds(k0, ts), :]], axis=1)
        for c, (st, hh) in enumerate(chains):
            if st < first_strip:
                continue
            s = lax.dot_general(kb, qs[st, hh], nt_dims, preferred_element_type=F32)
            s_ref[buf, c] = s
            bmax_ref[buf, c] = jnp.max(s, axis=0, keepdims=True)

    def softmax_pv(blk, buf, first_strip=0, diag_strip=None):
        vt = vt_ref[:, pl.ds(key_start(blk), ts)]
        vt_ones = [jnp.concatenate([vt[hh * HEAD_DIM:(hh + 1) * HEAD_DIM, :], ones_rows], axis=0)
                   for hh in range(2)]
        for c, (st, hh) in enumerate(chains):
            if st < first_strip:
                continue
            s = s_ref[buf, c]
            blk_max = bmax_ref[buf, c]
            if st == diag_strip:
                s = jnp.where(k_pos <= q_pos, s, MASK_VALUE)
                blk_max = jnp.max(s, axis=0, keepdims=True)
            m = m_ref[c]
            m_new = jnp.maximum(m, blk_max)
            alpha = jnp.exp2(m - m_new)
            p = jnp.exp2(s - m_new).astype(BF16)
            pv = jnp.dot(vt_ones[hh], p, preferred_element_type=F32)
            acc_ref[c] = alpha * acc_ref[c] + pv
            m_ref[c] = m_new

    m_ref[...] = jnp.full(m_ref.shape, MASK_VALUE, F32)
    acc_ref[...] = jnp.zeros_like(acc_ref)
    ones_rows = jnp.ones((BF16_SUBLANES, ts), BF16)

    scores(0, 0)

    def step(t, _):
        scores(2 * t + 1, 1)
        softmax_pv(2 * t, 0)
        scores(2 * t + 2, 0)
        softmax_pv(2 * t + 1, 1)
        return 0

    n_full = qi * n_strips
    lax.fori_loop(0, qi * (n_strips // 2), step, 0)
    for d in range(n_strips):
        if d + 1 < n_strips:
            scores(n_full + d + 1, (d + 1) % 2, first_strip=d + 1)
        softmax_pv(n_full + d, d % 2, first_strip=d, diag_strip=d)

    out_t = jnp.concatenate(
        [jnp.concatenate([acc_ref[c, 0:HEAD_DIM, :] / acc_ref[c, HEAD_DIM:HEAD_DIM + 1, :]
                          for c in range(st * 2, st * 2 + 2)], axis=0)
         for st in range(n_strips)], axis=1)
    o_ref[...] = out_t.T.astype(BF16)


def _attention(q, k, kc, vt):
    bsz, seq, w = q.shape
    pairs = w // LANES
    tq = T_ATTN
    ts = TS_ATTN
    n_chains = 2 * (tq // ts)
    return pl.pallas_call(
        _attn_kernel,
        name="fox_attn",
        grid=(bsz, pairs, seq // tq),
        in_specs=[
            pl.BlockSpec((None, tq, LANES), lambda b, p, i: (b, i, p)),
            pl.BlockSpec((None, seq, LANES), lambda b, p, i: (b, 0, p)),
            pl.BlockSpec((None, seq, LANES), lambda b, p, i: (b, 0, 0)),
            pl.BlockSpec((None, LANES, seq), lambda b, p, i: (b, p, 0)),
        ],
        out_specs=pl.BlockSpec((None, tq, LANES), lambda b, p, i: (b, i, p)),
        out_shape=jax.ShapeDtypeStruct(q.shape, BF16),
        scratch_shapes=[
            pltpu.VMEM((2, n_chains, ts, ts), F32),
            pltpu.VMEM((2, n_chains, 1, ts), F32),
            pltpu.VMEM((n_chains, 1, ts), F32),
            pltpu.VMEM((n_chains, HEAD_DIM + BF16_SUBLANES, ts), F32),
        ],
        compiler_params=pltpu.CompilerParams(
            dimension_semantics=("arbitrary", "arbitrary", "arbitrary"),
            vmem_limit_bytes=VMEM_LIMIT),
    )(q, k, kc, vt)


def _merge_kernel(x_ref, ya_ref, yb_ref, yc_ref, gpre_ref, wg_ref, bg_ref, wbr_ref, wout_ref,
                  gpost_ref, o_ref):
    x = x_ref[...]
    d = x.shape[1]
    hn = _rms(x, gpre_ref[...]).astype(BF16)
    merged = None
    for i, y_ref in enumerate((ya_ref, yb_ref, yc_ref)):
        cols = slice(i * d, (i + 1) * d)
        gate = jax.nn.sigmoid(
            jnp.dot(hn, wg_ref[:, cols], preferred_element_type=F32) + bg_ref[:, cols])
        br = jnp.dot(y_ref[...].astype(BF16), wbr_ref[i * BRANCH_WIDTH:(i + 1) * BRANCH_WIDTH, :],
                     preferred_element_type=F32)
        merged = gate * br if merged is None else merged + gate * br
    mix = jnp.dot(merged.astype(BF16), wout_ref[...], preferred_element_type=F32)
    o_ref[...] = x + _rms(mix, gpost_ref[...])


def _merge(layer, x, ya, yb, yc, g_pre, w_g, b_g, w_br, w_out, g_post):
    n, d = x.shape
    tm = TM_PROJ
    row_blk = lambda width: pl.BlockSpec((tm, width), lambda i: (i, 0))
    consts = (g_pre, w_g, b_g, w_br, w_out, g_post)
    return pl.pallas_call(
        _merge_kernel,
        name="merge",
        grid=(n // tm,),
        in_specs=[row_blk(d)] + [row_blk(BRANCH_WIDTH)] * 3 + [_layer_spec(c, layer) for c in consts],
        out_specs=row_blk(d),
        out_shape=jax.ShapeDtypeStruct(x.shape, F32),
        compiler_params=pltpu.CompilerParams(
            dimension_semantics=("arbitrary",),
            vmem_limit_bytes=VMEM_LIMIT),
    )(x, ya, yb, yc, *consts)


def _mlp_kernel(x_ref, gpre_ref, w1_ref, w2_ref, gpost_ref, o_ref):
    x = x_ref[...]
    d = x.shape[1]
    hn = _rms(x, gpre_ref[...]).astype(BF16)
    acc = None
    for c in range(w1_ref.shape[1] // d):
        cols = slice(c * d, (c + 1) * d)
        h1 = jnp.dot(hn, w1_ref[:, cols], preferred_element_type=F32)
        h1 = jnp.square(jnp.maximum(h1, 0.0)).astype(BF16)
        part = jnp.dot(h1, w2_ref[cols, :], preferred_element_type=F32)
        acc = part if acc is None else acc + part
    o_ref[...] = x + _rms(acc, gpost_ref[...])


def _mlp(layer, x, g_pre, w1, w2, g_post):
    n, d = x.shape
    tm = TM_PROJ
    blk = pl.BlockSpec((tm, d), lambda i: (i, 0))
    consts = (g_pre, w1, w2, g_post)
    return pl.pallas_call(
        _mlp_kernel,
        name="mlp",
        grid=(n // tm,),
        in_specs=[blk] + [_layer_spec(c, layer) for c in consts],
        out_specs=blk,
        out_shape=jax.ShapeDtypeStruct(x.shape, F32),
        compiler_params=pltpu.CompilerParams(
            dimension_semantics=("arbitrary",),
            vmem_limit_bytes=VMEM_LIMIT),
    )(x, *consts)


def kernel(x, g_pre_mix, w_in, b_gate, s5_a_re, s5_a_im, s5_log_dt, s5_b_re, s5_b_im, s5_c_re,
           s5_c_im, s5_d, s5_w_glu, conv_w, fox_b_f, w_branch, w_out, g_post_mix, g_pre_mlp,
           w_ff1, w_ff2, g_post_mlp):
    bsz, seq, d = x.shape
    depth = g_pre_mix.shape[0]
    n = bsz * seq

    n_act = 6 * BRANCH_WIDTH
    n_v = n_act + BRANCH_WIDTH
    w_in_b = w_in.astype(BF16)
    wv_t = jnp.swapaxes(w_in_b[:, :, n_act:n_v], 1, 2)
    wf_t = jnp.swapaxes(w_in_b[:, :, n_v:n_v + ATTN_HEADS], 1, 2)
    w_g = w_in_b[:, :, n_v + ATTN_HEADS:]
    w_br, w_o = w_branch.astype(BF16), w_out.astype(BF16)
    w1, w2 = w_ff1.astype(BF16), w_ff2.astype(BF16)
    row = lambda v: v.astype(F32).reshape(depth, 1, -1)
    g_mix, g_post, g_mlp, g_post2, b_g = map(
        row, (g_pre_mix, g_post_mix, g_pre_mlp, g_post_mlp, b_gate))
    b_f = fox_b_f.astype(F32).reshape(depth, ATTN_HEADS, 1)
    cw = conv_w.astype(F32)
    s5_params = _s5_params(s5_a_re, s5_a_im, s5_log_dt, s5_b_re, s5_b_im, s5_c_re, s5_c_im,
                           s5_d, s5_w_glu)

    flat = lambda t: t.reshape(n, t.shape[-1])
    for layer in range(depth):
        ut, yb, q, k, vt, kc = _in_proj(layer, x, g_mix, w_in_b, wv_t, wf_t, b_f, cw)
        ya = _s5(layer, ut, s5_params)
        yc = _attention(q, k, kc, vt)
        x1 = _merge(layer, flat(x), flat(ya), flat(yb), flat(yc), g_mix, w_g, b_g, w_br, w_o,
                    g_post)
        x = _mlp(layer, x1, g_mlp, w1, w2, g_post2).reshape(bsz, seq, d)
    return x
```

```python
import functools
import math

import jax
import jax.numpy as jnp
from jax import lax
from jax.experimental import pallas as pl
from jax.experimental.pallas import tpu as pltpu

F32 = jnp.float32
BF16 = jnp.bfloat16

EPS = 1e-6
MASK_VALUE = -1e30

ATTN_HEADS = 8
HEAD_DIM = 64
CONV_K = 3
BRANCH_WIDTH = 512

LANES = 128
SUBLANES = 8
BF16_SUBLANES = 16
VMEM_LIMIT = 56 * 1024 * 1024

TM_PROJ = 1024
S5_RUN = 16
S5_T = 16
S5_CBLK = 64
T_ATTN = 1024
TS_ATTN = 256
LOG2E = math.log2(math.e)
CUM_PIECES = 3


def _rms(x, g):
    return x * lax.rsqrt(jnp.mean(x * x, axis=-1, keepdims=True) + EPS) * g


def _transpose_runs(arrs):
    n = len(arrs)
    assert n * S5_RUN == LANES
    run = lax.broadcasted_iota(jnp.int32, arrs[0].shape, 1) // S5_RUN
    d = n // 2
    while d:
        low = (run & d) == 0
        nxt = list(arrs)
        for i in range(n):
            if not i & d:
                a, b = arrs[i], arrs[i + d]
                nxt[i] = jnp.where(low, a, pltpu.roll(b, S5_RUN * d, 1))
                nxt[i + d] = jnp.where(low, pltpu.roll(a, LANES - S5_RUN * d, 1), b)
        arrs = nxt
        d //= 2
    return arrs


def _cumsum_lanes(x):
    n = x.shape[-1]
    lane = lax.broadcasted_iota(jnp.int32, x.shape, x.ndim - 1)
    shift = 1
    while shift < n:
        x = x + jnp.where(lane >= shift, pltpu.roll(x, shift, x.ndim - 1), 0.0)
        shift *= 2
    return x


def _layer_spec(arr, layer, block=None):
    block = arr.shape[1:] if block is None else block
    return pl.BlockSpec((None,) + tuple(block), lambda *_: (layer,) + (0,) * len(block),
                        pipeline_mode=pl.Buffered(1))


def _inproj_kernel(x_ref, g_ref, w_ref, wvt_ref, wft_ref, bf_ref, cw_ref,
                   ut_ref, yb_ref, q_ref, k_ref, vt_ref, kc_ref,
                   vtail_ref, carry_ref, uscr_ref):
    j = pl.program_id(1)
    tm = x_ref.shape[0]
    w = BRANCH_WIDTH

    @pl.when(j == 0)
    def _():
        vtail_ref[...] = jnp.zeros_like(vtail_ref)
        carry_ref[...] = jnp.zeros_like(carry_ref)

    hn = _rms(x_ref[...], g_ref[...]).astype(BF16)

    def proj(c):
        return jnp.dot(hn, w_ref[:, c * w:(c + 1) * w], preferred_element_type=F32)

    u = proj(0)
    n_lt = w // LANES
    for lt in range(n_lt):
        uscr_ref[lt] = u[:, lt * LANES:(lt + 1) * LANES]
    n_sub = tm // S5_T
    runs_per_tile = LANES // S5_RUN
    for lt in range(n_lt):
        for t8 in range(S5_T // runs_per_tile):
            xs = [uscr_ref[lt, pl.ds(t8 * runs_per_tile + tt, n_sub, stride=S5_T), :]
                  for tt in range(runs_per_tile)]
            for gi, flat in enumerate(_transpose_runs(xs)):
                ut_ref[lt * runs_per_tile + gi, :, t8 * LANES:(t8 + 1) * LANES] = flat.astype(BF16)

    vv = proj(3) * proj(1)
    tail = vtail_ref[...]
    row = lax.broadcasted_iota(jnp.int32, vv.shape, 0)
    v1 = jnp.where(row == 0, tail[7:8, :], pltpu.roll(vv, 1, 0))
    v2 = jnp.where(row == 0, tail[6:7, :],
                   jnp.where(row == 1, tail[7:8, :], pltpu.roll(vv, 2, 0)))
    cw = cw_ref[...]
    conv = v2 * cw[0:1, :] + v1 * cw[1:2, :] + vv * cw[2:3, :]
    yb_ref[...] = (proj(2) * conv).astype(BF16)
    vtail_ref[...] = vv[tm - SUBLANES:tm, :]

    q_ref[...] = (proj(4) * (HEAD_DIM ** -0.5 * LOG2E)).astype(BF16)
    k_ref[...] = proj(5).astype(BF16)
    nt_dims = (((1,), (1,)), ((), ()))
    vt_ref[...] = lax.dot_general(wvt_ref[...], hn, nt_dims,
                                  preferred_element_type=F32).astype(BF16)

    fl = lax.dot_general(wft_ref[...], hn, nt_dims, preferred_element_type=F32)
    z = fl + bf_ref[...]
    log_f = (jnp.minimum(z, 0.0) - jnp.log1p(jnp.exp(-jnp.abs(z)))) * LOG2E
    cum = _cumsum_lanes(log_f) + carry_ref[:, 0:1]
    carry_ref[...] = jnp.broadcast_to(cum[:, tm - 1:tm], carry_ref.shape)
    hi = cum.astype(BF16).astype(F32)
    mid = (cum - hi).astype(BF16).astype(F32)
    lo = cum - hi - mid
    pieces = jnp.concatenate(
        [hi, mid, lo, jnp.zeros((LANES - CUM_PIECES * ATTN_HEADS, tm), F32)], axis=0)
    kc_ref[...] = pieces.T.astype(BF16)


def _in_proj(layer, x, g, w_in, wv_t, wf_t, b_f, conv_w):
    bsz, seq, d = x.shape
    tm = TM_PROJ
    n_act = 6 * BRANCH_WIDTH
    n_groups = BRANCH_WIDTH // S5_RUN
    act =jax.ShapeDtypeStruct((bsz, seq, BRANCH_WIDTH), BF16)
    act_spec = pl.BlockSpec((None, tm, BRANCH_WIDTH), lambda b, j: (b, j, 0))
    return pl.pallas_call(
        _inproj_kernel,
        name="in_proj",
        grid=(bsz, seq // tm),
        in_specs=[
            pl.BlockSpec((None, tm, d), lambda b, j: (b, j, 0)),
            _layer_spec(g, layer),
            _layer_spec(w_in, layer, (d, n_act)),
            _layer_spec(wv_t, layer),
            _layer_spec(wf_t, layer),
            _layer_spec(b_f, layer),
            _layer_spec(conv_w, layer),
        ],
        out_specs=[
            pl.BlockSpec((n_groups, None, tm // S5_T, S5_T * S5_RUN), lambda b, j: (0, b, j, 0))
        ] + [act_spec] * 3 + [
            pl.BlockSpec((None, BRANCH_WIDTH, tm), lambda b, j: (b, 0, j)),
            pl.BlockSpec((None, tm, LANES), lambda b, j: (b, j, 0))],
        out_shape=[
            jax.ShapeDtypeStruct((n_groups, bsz, seq // S5_T, S5_T * S5_RUN), BF16)
        ] + [act] * 3 + [
            jax.ShapeDtypeStruct((bsz, BRANCH_WIDTH, seq), BF16),
            jax.ShapeDtypeStruct((bsz, seq, LANES), BF16)],
        scratch_shapes=[pltpu.VMEM((SUBLANES, BRANCH_WIDTH), F32),
                        pltpu.VMEM((ATTN_HEADS, LANES), F32),
                        pltpu.VMEM((BRANCH_WIDTH // LANES, tm, LANES), F32)],
        compiler_params=pltpu.CompilerParams(
            dimension_semantics=("arbitrary", "arbitrary"),
            vmem_limit_bytes=VMEM_LIMIT),
    )(x, g, w_in, wv_t, wf_t, b_f, conv_w)


def _gelu_tanh(x):
    c = math.sqrt(2.0 / math.pi)
    return 0.5 * x * (1.0 + jnp.tanh(c * (x + 0.044715 * (x * x * x))))


def _s5_kernel(u_ref, kern_ref, wre_ref, wim_ref, vre_ref, vim_ref, are_ref, aim_ref, d_ref,
               glu0_ref, y_ref, xre_ref, xim_ref, stre_ref, stim_ref, yt_ref, m_ref, glu_ref):
    j = pl.program_id(1)
    gb, bsz, n_chunks, width = u_ref.shape
    n_pairs = gb // 2
    rows = bsz * n_chunks

    @pl.when(j == 0)
    def _():
        stre_ref[...] = jnp.zeros_like(stre_ref)
        stim_ref[...] = jnp.zeros_like(stim_ref)
        lane = lax.broadcasted_iota(jnp.int32, (S5_RUN, width), 1)
        for g in range(gb):
            k_slab = kern_ref[g]
            w_slab = glu0_ref[g]
            for jj in range(S5_T):
                rows_j = slice(jj * S5_RUN, (jj + 1) * S5_RUN)
                k_j = pltpu.roll(k_slab, jj * S5_RUN, 1) if jj else k_slab
                w_j = pltpu.roll(w_slab, jj * S5_RUN, 1) if jj else w_slab
                m_ref[g, rows_j, :] = jnp.where(lane >= jj * S5_RUN, k_j, 0.0).astype(BF16)
                glu_ref[g, rows_j, :] = w_j.astype(BF16)

    def u_rows(g):
        return u_ref[g].reshape(rows, width)

    for pr in range(n_pairs):
        u_pair = jnp.concatenate([u_rows(2 * pr), u_rows(2 * pr + 1)], axis=1)
        xre_ref[pr] = jnp.dot(u_pair, wre_ref[pr], preferred_element_type=F32)
        xim_ref[pr] = jnp.dot(u_pair, wim_ref[pr], preferred_element_type=F32)

    def step(c, carry):
        new = []
        for pr in range(n_pairs):
            sr, si = carry[pr]
            r = pl.ds(c, bsz, stride=n_chunks)
            xr = xre_ref[pr, r, :]
            xi = xim_ref[pr, r, :]
            xre_ref[pr, r, :] = sr
            xim_ref[pr, r, :] = si
            ar = are_ref[:, pr * LANES:(pr + 1) * LANES]
            ai = aim_ref[:, pr * LANES:(pr + 1) * LANES]
            new.append((ar * sr - ai * si + xr, ar * si + ai * sr + xi))
        return tuple(new)

    state = lax.fori_loop(0, n_chunks, step,
                          tuple((stre_ref[pr], stim_ref[pr]) for pr in range(n_pairs)), unroll=4)
    for pr in range(n_pairs):
        stre_ref[pr], stim_ref[pr] = state[pr]

    for g in range(gb):
        u = u_rows(g)
        y = jnp.dot(u, m_ref[g], preferred_element_type=F32)
        y = y + jnp.dot(xre_ref[g // 2].astype(BF16), vre_ref[g], preferred_element_type=F32)
        y = y + jnp.dot(xim_ref[g // 2].astype(BF16), vim_ref[g], preferred_element_type=F32)
        y = _gelu_tanh(y + d_ref[g] * u.astype(F32))
        gate = jnp.dot(y.astype(BF16), glu_ref[g], preferred_element_type=F32)
        y = y * jax.nn.sigmoid(gate)
        for t8 in range(width // LANES):
            yt_ref[g, t8] = y[:, t8 * LANES:(t8 + 1) * LANES]

    for b in range(bsz):
        for t8 in range(width // LANES):
            zs = [yt_ref[g, t8, b * n_chunks:(b + 1) * n_chunks, :] for g in range(gb)]
            for tt, rows_t in enumerate(_transpose_runs(zs)):
                y_ref[b, pl.ds(t8 * gb + tt, n_chunks, stride=S5_T), :] = rows_t


def _s5_params(a_re, a_im, log_dt, b_re, b_im, c_re, c_im, d_skip, w_glu):
    depth, g, p, h = b_re.shape
    t_sub = S5_T
    f = lambda v: v.astype(F32)
    tr = lambda v: jnp.swapaxes(f(v), -1, -2)
    lam_re = jnp.minimum(f(a_re), -1e-4)
    lam_im = f(a_im)
    dt = jnp.exp(f(log_dt))[..., None]
    rho, theta = lam_re * dt, lam_im * dt

    abar_re = jnp.exp(rho) * jnp.cos(theta)
    abar_im = jnp.exp(rho) * jnp.sin(theta)
    inv = 1.0 / (lam_re * lam_re + lam_im * lam_im)
    coef_re = (((abar_re - 1.0) * lam_re + abar_im * lam_im) * inv)[:, :, None, :]
    coef_im = ((abar_im * lam_re - (abar_re - 1.0) * lam_im) * inv)[:, :, None, :]
    bbar_re = coef_re * tr(b_re) - coef_im * tr(b_im)
    bbar_im = coef_re * tr(b_im) + coef_im * tr(b_re)

    tau = jnp.arange(t_sub + 1, dtype=F32)[:, None]
    pw_mag = jnp.exp(tau * rho[:, :, None, :])
    pw_re = pw_mag * jnp.cos(tau * theta[:, :, None, :])
    pw_im = pw_mag * jnp.sin(tau * theta[:, :, None, :])

    pr, pi = pw_re[:, :, :t_sub, None, :], pw_im[:, :, :t_sub, None, :]
    ab_re = pr * bbar_re[:, :, None] - pi * bbar_im[:, :, None]
    ab_im = pr * bbar_im[:, :, None] + pi * bbar_re[:, :, None]

    kern = (jnp.einsum('dgop,dgthp->dghto', f(c_re), ab_re)
            - jnp.einsum('dgop,dgthp->dghto', f(c_im), ab_im)).reshape(depth, g, h, t_sub * h)

    def pair_block_diag(w):
        w = jnp.flip(w, axis=2).reshape(depth, g // 2, 2, t_sub * h, p)
        return jnp.einsum('dqirc,ij->dqirjc', w, jnp.eye(2, dtype=F32)).reshape(
            depth, g // 2, 2 * t_sub * h, 2 * p).astype(BF16)

    w_re, w_im = pair_block_diag(ab_re), pair_block_diag(ab_im)

    p1_re, p1_im = tr(pw_re[:, :, 1:])[..., None], tr(pw_im[:, :, 1:])[..., None]
    ct_re, ct_im = tr(c_re)[:, :, :, None, :], tr(c_im)[:, :, :, None, :]

    def pair_rows(v):
        v = v.reshape(depth, g // 2, 2, p, t_sub * h)
        return jnp.einsum('dqipc,ij->dqijpc', v, jnp.eye(2, dtype=F32)).reshape(
            depth, g, 2 * p, t_sub * h).astype(BF16)

    v_re = pair_rows(ct_re * p1_re - ct_im * p1_im)
    v_im = pair_rows(-(ct_re * p1_im + ct_im * p1_re))

    at_re = jnp.broadcast_to(pw_re[:, :, t_sub].reshape(depth, 1, g * p), (depth, SUBLANES, g * p))
    at_im = jnp.broadcast_to(pw_im[:, :, t_sub].reshape(depth, 1, g * p), (depth, SUBLANES, g * p))

    d_t = jnp.tile(f(d_skip)[:, :, None, :], (1, 1, t_sub, 1)).reshape(depth, g, 1, t_sub * h)
    glu0 = jnp.pad(f(w_glu), ((0, 0), (0, 0), (0, 0), (0, (t_sub - 1) * h)))
    return kern, w_re, w_im, v_re, v_im, at_re, at_im, d_t, glu0


def _s5(layer, u_t, params):
    g, bsz, n_chunks, width = u_t.shape
    kern, w_re, w_im, v_re, v_im, at_re, at_im, d_t, glu0 = params
    gb = LANES // S5_RUN
    cb = S5_CBLK
    rows = bsz * cb
    n_pairs = gb // 2
    assert bsz == SUBLANES and width == 2 * LANES and S5_T == 2 * gb

    def per_group(arr, groups):
        blk = (None, groups) + arr.shape[2:]
        return pl.BlockSpec(blk, lambda i, j: (layer, i) + (0,) * (arr.ndim - 2))

    decay_spec = pl.BlockSpec((None, SUBLANES, n_pairs * LANES), lambda i, j: (layer, 0, i))
    return pl.pallas_call(
        _s5_kernel,
        name="s5",
        grid=(g // gb, n_chunks // cb),
        in_specs=[pl.BlockSpec((gb, bsz, cb, width), lambda i, j: (i, 0, j, 0)),
                  per_group(kern, gb), per_group(w_re, n_pairs), per_group(w_im, n_pairs),
                  per_group(v_re, gb), per_group(v_im, gb), decay_spec, decay_spec,
                  per_group(d_t, gb), per_group(glu0, gb)],
        out_specs=pl.BlockSpec((bsz, cb * S5_T, LANES), lambda i, j: (0, j, i)),
        out_shape=jax.ShapeDtypeStruct((bsz, n_chunks * S5_T, g * S5_RUN), F32),
        scratch_shapes=[
            pltpu.VMEM((n_pairs, rows, LANES), F32),
            pltpu.VMEM((n_pairs, rows, LANES), F32),
            pltpu.VMEM((n_pairs, SUBLANES, LANES), F32),
            pltpu.VMEM((n_pairs, SUBLANES, LANES), F32),
            pltpu.VMEM((gb, width // LANES, rows, LANES), F32),
            pltpu.VMEM((gb, width, width), BF16),
            pltpu.VMEM((gb, width, width), BF16),
        ],
        compiler_params=pltpu.CompilerParams(
            dimension_semantics=("arbitrary", "arbitrary"),
            vmem_limit_bytes=VMEM_LIMIT),
    )(u_t, kern, w_re, w_im, v_re, v_im, at_re, at_im, d_t, glu0)


def _attn_kernel(q_ref, k_ref, kc_ref, vt_ref, o_ref, s_ref, bmax_ref, m_ref, acc_ref, *,
                 n_q_blocks):
    pair = pl.program_id(1)
    qi = pl.program_id(2)
    tq = q_ref.shape[0]
    ts = TS_ATTN
    n_strips = tq // ts
    assert n_strips % 2 == 0
    q = q_ref[...]
    lane = lax.broadcasted_iota(jnp.int32, (ts, LANES), 1)
    k_pos = lax.broadcasted_iota(jnp.int32, (ts, ts), 0)
    q_pos = lax.broadcasted_iota(jnp.int32, (ts, ts), 1)
    nt_dims = (((1,), (1,)), ((), ()))

    chains = [(st, hh) for st in range(n_strips) for hh in range(2)]
    qs = {}
    for st, hh in chains:
        q_st = q[st * ts:(st + 1) * ts, :]
        head_lanes = (lane >= HEAD_DIM) if hh else (lane < HEAD_DIM)
        piece_lanes = (lane < CUM_PIECES * ATTN_HEADS) & (lane % ATTN_HEADS == 2 * pair + hh)
        qs[st, hh] = jnp.concatenate(
            [jnp.where(head_lanes, q_st, jnp.zeros_like(q_st)),
             jnp.where(piece_lanes, -1.0, 0.0).astype(BF16)], axis=1)

    def key_start(blk):
        return blk * ts

    def scores(blk, buf, first_strip=0):
        k0 = key_start(blk)
        kb = jnp.concatenate([k_ref[pl.ds(k0, ts), :], kc_ref[pl.ds(k0, ts), :]], axis=1)
        for c, (st, hh) in enumerate(chains):
            if st < first_strip:
                continue
            s = lax.dot_general(kb, qs[st, hh], nt_dims, preferred_element_type=F32)
            s_ref[buf, c] = s
            bmax_ref[buf, c] = jnp.max(s, axis=0, keepdims=True)

    def softmax_pv(blk, buf, first_strip=0, diag_strip=None):
        vt = vt_ref[:, pl.ds(key_start(blk), ts)]
        vt_ones = [jnp.concatenate([vt[hh * HEAD_DIM:(hh + 1) * HEAD_DIM, :], ones_rows], axis=0)
                   for hh in range(2)]
        for c, (st, hh) in enumerate(chains):
            if st < first_strip:
                continue
            s = s_ref[buf, c]
            blk_max = bmax_ref[buf, c]
            if st == diag_strip:
                s = jnp.where(k_pos <= q_pos, s, MASK_VALUE)
                blk_max = jnp.max(s, axis=0, keepdims=True)
            m = m_ref[c]
            m_new = jnp.maximum(m, blk_max)
            alpha = jnp.exp2(m - m_new)
            p = jnp.exp2(s - m_new).astype(BF16)
            pv = jnp.dot(vt_ones[hh], p, preferred_element_type=F32)
            acc_ref[c] = alpha * acc_ref[c] + pv
            m_ref[c] = m_new

    m_ref[...] = jnp.full(m_ref.shape, MASK_VALUE, F32)
    acc_ref[...] = jnp.zeros_like(acc_ref)
    ones_rows = jnp.ones((BF16_SUBLANES, ts), BF16)

    def run(n_full):
        scores(0, 0)
        for t in range(n_full // 2):
            scores(2 * t + 1, 1)
            softmax_pv(2 * t, 0)
            scores(2 * t + 2, 0)
            softmax_pv(2 * t + 1, 1)
        for d in range(n_strips):
            if d + 1 < n_strips:
                scores(n_full + d + 1, (d + 1) % 2, first_strip=d + 1)
            softmax_pv(n_full + d, d % 2, first_strip=d, diag_strip=d)

    for q_blk in range(n_q_blocks):
        pl.when(qi == q_blk)(lambda q_blk=q_blk: run(q_blk * n_strips))

    out_t = jnp.concatenate(
        [jnp.concatenate([acc_ref[c, 0:HEAD_DIM, :] / acc_ref[c, HEAD_DIM:HEAD_DIM + 1, :]
                          for c in range(st * 2, st * 2 + 2)], axis=0)
         for st in range(n_strips)], axis=1)
    o_ref[...] = out_t.T.astype(BF16)


def _attention(q, k, kc, vt):
    bsz, seq, w = q.shape
    pairs = w // LANES
    tq = T_ATTN
    ts = TS_ATTN
    n_chains = 2 * (tq // ts)
    return pl.pallas_call(
        functools.partial(_attn_kernel, n_q_blocks=seq // tq),
        name="fox_attn",
        grid=(bsz, pairs, seq // tq),
        in_specs=[
            pl.BlockSpec((None, tq, LANES), lambda b, p, i: (b, i, p)),
            pl.BlockSpec((None, seq, LANES), lambda b, p, i: (b, 0, p)),
            pl.BlockSpec((None, seq, LANES), lambda b, p, i: (b, 0, 0)),
            pl.BlockSpec((None, LANES, seq), lambda b, p, i: (b, p, 0)),
        ],
        out_specs=pl.BlockSpec((None, tq, LANES), lambda b, p, i: (b, i, p)),
        out_shape=jax.ShapeDtypeStruct(q.shape, BF16),
        scratch_shapes=[
            pltpu.VMEM((2, n_chains, ts, ts), F32),
            pltpu.VMEM((2, n_chains, 1, ts), F32),
            pltpu.VMEM((n_chains, 1, ts), F32),
            pltpu.VMEM((n_chains, HEAD_DIM + BF16_SUBLANES, ts), F32),
        ],
        compiler_params=pltpu.CompilerParams(
            dimension_semantics=("arbitrary", "arbitrary", "arbitrary"),
            vmem_limit_bytes=VMEM_LIMIT),
    )(q, k, kc, vt)


def _merge_kernel(x_ref, ya_ref, yb_ref, yc_ref, gpre_ref, wg_ref, bg_ref, wbr_ref, wout_ref,
                  gpost_ref, o_ref):
    x = x_ref[...]
    d = x.shape[1]
    hn = _rms(x, gpre_ref[...]).astype(BF16)
    merged = None
    for i, y_ref in enumerate((ya_ref, yb_ref, yc_ref)):
        cols = slice(i * d, (i + 1) * d)
        gate = jax.nn.sigmoid(
            jnp.dot(hn, wg_ref[:, cols], preferred_element_type=F32) + bg_ref[:, cols])
        br = jnp.dot(y_ref[...].astype(BF16), wbr_ref[i * BRANCH_WIDTH:(i + 1) * BRANCH_WIDTH, :],
                     preferred_element_type=F32)
        merged = gate * br if merged is None else merged + gate * br
    mix = jnp.dot(merged.astype(BF16), wout_ref[...], preferred_element_type=F32)
    o_ref[...] = x + _rms(mix, gpost_ref[...])


def _merge(layer, x, ya, yb, yc, g_pre, w_g, b_g, w_br, w_out, g_post):
    n, d = x.shape
    tm = TM_PROJ
    row_blk = lambda width: pl.BlockSpec((tm, width), lambda i: (i, 0))
    consts = (g_pre, w_g, b_g, w_br, w_out, g_post)
    return pl.pallas_call(
        _merge_kernel,
        name="merge",
        grid=(n // tm,),
        in_specs=[row_blk(d)] + [row_blk(BRANCH_WIDTH)] * 3 + [_layer_spec(c, layer) for c in consts],
        out_specs=row_blk(d),
        out_shape=jax.ShapeDtypeStruct(x.shape, F32),
        compiler_params=pltpu.CompilerParams(
            dimension_semantics=("arbitrary",),
            vmem_limit_bytes=VMEM_LIMIT),
    )(x, ya, yb, yc, *consts)


def _mlp_kernel(x_ref, gpre_ref, w1_ref, w2_ref, gpost_ref, o_ref):
    x = x_ref[...]
    d = x.shape[1]
    hn = _rms(x, gpre_ref[...]).astype(BF16)
    acc = None
    for c in range(w1_ref.shape[1] // d):
        cols = slice(c * d, (c + 1) * d)
        h1 = jnp.dot(hn, w1_ref[:, cols], preferred_element_type=F32)
        h1 = jnp.square(jnp.maximum(h1, 0.0)).astype(BF16)
        part = jnp.dot(h1, w2_ref[cols, :], preferred_element_type=F32)
        acc = part if acc is None else acc + part
    o_ref[...] = x + _rms(acc, gpost_ref[...])


def _mlp(layer, x, g_pre, w1, w2, g_post):
    n, d = x.shape
    tm = TM_PROJ
    blk = pl.BlockSpec((tm, d), lambda i: (i, 0))
    consts = (g_pre, w1, w2, g_post)
    return pl.pallas_call(
        _mlp_kernel,
        name="mlp",
        grid=(n // tm,),
        in_specs=[blk] + [_layer_spec(c, layer) for c in consts],
        out_specs=blk,
        out_shape=jax.ShapeDtypeStruct(x.shape, F32),
        compiler_params=pltpu.CompilerParams(
            dimension_semantics=("arbitrary",),
            vmem_limit_bytes=VMEM_LIMIT),
    )(x, *consts)


def kernel(x, g_pre_mix, w_in, b_gate, s5_a_re, s5_a_im, s5_log_dt, s5_b_re, s5_b_im, s5_c_re,
           s5_c_im, s5_d, s5_w_glu, conv_w, fox_b_f, w_branch, w_out, g_post_mix, g_pre_mlp,
           w_ff1, w_ff2, g_post_mlp):
    bsz, seq, d = x.shape
    depth = g_pre_mix.shape[0]
    n = bsz * seq

    n_act = 6 * BRANCH_WIDTH
    n_v = n_act + BRANCH_WIDTH
    w_in_b = w_in.astype(BF16)
    wv_t = jnp.swapaxes(w_in_b[:, :, n_act:n_v], 1, 2)
    wf_t = jnp.swapaxes(w_in_b[:, :, n_v:n_v + ATTN_HEADS], 1, 2)
    w_g = w_in_b[:, :, n_v + ATTN_HEADS:]
    w_br, w_o = w_branch.astype(BF16), w_out.astype(BF16)
    w1, w2 = w_ff1.astype(BF16), w_ff2.astype(BF16)
    row = lambda v: v.astype(F32).reshape(depth, 1, -1)
    g_mix, g_post, g_mlp, g_post2, b_g = map(
        row, (g_pre_mix, g_post_mix, g_pre_mlp, g_post_mlp, b_gate))
    b_f = fox_b_f.astype(F32).reshape(depth, ATTN_HEADS, 1)
    cw = conv_w.astype(F32)
    s5_params = _s5_params(s5_a_re, s5_a_im, s5_log_dt, s5_b_re, s5_b_im, s5_c_re, s5_c_im,
                           s5_d, s5_w_glu)

    flat = lambda t: t.reshape(n, t.shape[-1])
    for layer in range(depth):
        ut, yb, q, k, vt, kc = _in_proj(layer, x, g_mix, w_in_b, wv_t, wf_t, b_f, cw)
        ya = _s5(layer, ut, s5_params)
        yc = _attention(q, k, kc, vt)
        x1 = _merge(layer, flat(x), flat(ya), flat(yb), flat(yc), g_mix, w_g, b_g, w_br, w_o,
                    g_post)
        x = _mlp(layer, x1, g_mlp, w1, w2, g_post2).reshape(bsz, seq, d)
    return x
```

```python
import functools
import math

import jax
import jax.numpy as jnp
from jax import lax
from jax.experimental import pallas as pl
from jax.experimental.pallas import tpu as pltpu

F32 = jnp.float32
BF16 = jnp.bfloat16

EPS = 1e-6
MASK_VALUE = -1e30

ATTN_HEADS = 8
HEAD_DIM = 64
CONV_K = 3
BRANCH_WIDTH = 512

LANES = 128
SUBLANES = 8
BF16_SUBLANES = 16
VMEM_LIMIT = 56 * 1024 * 1024

TM_PROJ = 1024
S5_RUN = 16
S5_T = 16
S5_CBLK = 64
T_ATTN = 1024
TS_ATTN = 256
LOG2E = math.log2(math.e)
CUM_PIECES = 3


def _rms(x, g):
    return x * lax.rsqrt(jnp.mean(x * x, axis=-1, keepdims=True) + EPS) * g


def _transpose_runs(arrs):
    n = len(arrs)
    assert n * S5_RUN == LANES
    run = lax.broadcasted_iota(jnp.int32, arrs[0].shape, 1) // S5_RUN
    d = n // 2
    while d:
        low = (run & d) == 0
        nxt = list(arrs)
        for i in range(n):
            if not i & d:
                a, b = arrs[i], arrs[i + d]
                nxt[i] = jnp.where(low, a, pltpu.roll(b, S5_RUN * d, 1))
                nxt[i + d] = jnp.where(low, pltpu.roll(a, LANES - S5_RUN * d, 1), b)
        arrs = nxt
        d //= 2
    return arrs


def _cumsum_lanes(x):
    n = x.shape[-1]
    lane = lax.broadcasted_iota(jnp.int32, x.shape, x.ndim - 1)
    shift = 1
    while shift < n:
        x = x + jnp.where(lane >= shift, pltpu.roll(x, shift, x.ndim - 1), 0.0)
        shift *= 2
    return x


def _layer_spec(arr, layer, block=None):
    block = arr.shape[1:] if block is None else block
    return pl.BlockSpec((None,) + tuple(block), lambda *_: (layer,) + (0,) * len(block),
                        pipeline_mode=pl.Buffered(1))


def _inproj_kernel(x_ref, g_ref, w_ref, wvt_ref, wft_ref, bf_ref, cw_ref,
                   ut_ref, yb_ref, q_ref, k_ref, vt_ref, kc_ref,
                   vtail_ref, carry_ref, uscr_ref):
    j = pl.program_id(1)
    tm = x_ref.shape[0]
    w = BRANCH_WIDTH

    @pl.when(j == 0)
    def _():
        vtail_ref[...] = jnp.zeros_like(vtail_ref)
        carry_ref[...] = jnp.zeros_like(carry_ref)

    hn = _rms(x_ref[...], g_ref[...]).astype(BF16)

    def proj(c):
        return jnp.dot(hn, w_ref[:, c * w:(c + 1) * w], preferred_element_type=F32)


    nt_dims = (((1,), (1,)), ((), ()))
    fl = lax.dot_general(wft_ref[...], hn, nt_dims, preferred_element_type=F32)
    z = fl + bf_ref[...]
    log_f = (jnp.minimum(z, 0.0) - jnp.log1p(jnp.exp(-jnp.abs(z)))) * LOG2E
    cum = _cumsum_lanes(log_f) + carry_ref[:, 0:1]
    carry_ref[...] = jnp.broadcast_to(cum[:, tm - 1:tm], carry_ref.shape)
    hi = cum.astype(BF16).astype(F32)
    mid = (cum - hi).astype(BF16).astype(F32)
    lo = cum - hi - mid
    pieces = jnp.concatenate(
        [hi, mid, lo, jnp.zeros((LANES - CUM_PIECES * ATTN_HEADS, tm), F32)], axis=0)
    kc_ref[...] = pieces.T.astype(BF16)

    u = proj(0)
    n_lt = w // LANES
    for lt in range(n_lt):
        uscr_ref[lt] = u[:, lt * LANES:(lt + 1) * LANES]
    n_sub = tm // S5_T
    runs_per_tile = LANES // S5_RUN
    for lt in range(n_lt):
        for t8 in range(S5_T // runs_per_tile):
            xs = [uscr_ref[lt, pl.ds(t8 * runs_per_tile + tt, n_sub, stride=S5_T), :]
                  for tt in range(runs_per_tile)]
            for gi, flat in enumerate(_transpose_runs(xs)):
                ut_ref[lt * runs_per_tile + gi, :, t8 * LANES:(t8 + 1) * LANES] = flat.astype(BF16)

    vv = proj(3) * proj(1)
    tail = vtail_ref[...]
    row = lax.broadcasted_iota(jnp.int32, vv.shape, 0)
    v1 = jnp.where(row == 0, tail[7:8, :], pltpu.roll(vv, 1, 0))
    v2 = jnp.where(row == 0, tail[6:7, :],
                   jnp.where(row == 1, tail[7:8, :], pltpu.roll(vv, 2, 0)))
    cw = cw_ref[...]
    conv = v2 * cw[0:1, :] + v1 * cw[1:2, :] + vv * cw[2:3, :]
    yb_ref[...] = (proj(2) * conv).astype(BF16)
    vtail_ref[...] = vv[tm - SUBLANES:tm, :]

    q_ref[...] = (proj(4) * (HEAD_DIM ** -0.5 * LOG2E)).astype(BF16)
    vt_ref[...] = lax.dot_general(wvt_ref[...], hn, nt_dims,
                                  preferred_element_type=F32).astype(BF16)
    k_ref[...] = proj(5).astype(BF16)


def _in_proj(layer, x, g, w_in, wv_t, wf_t, b_f, conv_w):
    bsz, seq, d = x.shape
    tm = TM_PROJ
    n_act = 6 * BRANCH_WIDTH
    n_groups = BRANCH_WIDTH // S5_RUN
    act =jax.ShapeDtypeStruct((bsz, seq, BRANCH_WIDTH), BF16)
    act_spec = pl.BlockSpec((None, tm, BRANCH_WIDTH), lambda b, j: (b, j, 0))
    return pl.pallas_call(
        _inproj_kernel,
        name="in_proj",
        grid=(bsz, seq // tm),
        in_specs=[
            pl.BlockSpec((None, tm, d), lambda b, j: (b, j, 0)),
            _layer_spec(g, layer),
            _layer_spec(w_in, layer, (d, n_act)),
            _layer_spec(wv_t, layer),
            _layer_spec(wf_t, layer),
            _layer_spec(b_f, layer),
            _layer_spec(conv_w, layer),
        ],
        out_specs=[
            pl.BlockSpec((n_groups, None, tm // S5_T, S5_T * S5_RUN), lambda b, j: (0, b, j, 0))
        ] + [act_spec] * 3 + [
            pl.BlockSpec((None, BRANCH_WIDTH, tm), lambda b, j: (b, 0, j)),
            pl.BlockSpec((None, tm, LANES), lambda b, j: (b, j, 0))],
        out_shape=[
            jax.ShapeDtypeStruct((n_groups, bsz, seq // S5_T, S5_T * S5_RUN), BF16)
        ] + [act] * 3 + [
            jax.ShapeDtypeStruct((bsz, BRANCH_WIDTH, seq), BF16),
            jax.ShapeDtypeStruct((bsz, seq, LANES), BF16)],
        scratch_shapes=[pltpu.VMEM((SUBLANES, BRANCH_WIDTH), F32),
                        pltpu.VMEM((ATTN_HEADS, LANES), F32),
                        pltpu.VMEM((BRANCH_WIDTH // LANES, tm, LANES), F32)],
        compiler_params=pltpu.CompilerParams(
            dimension_semantics=("arbitrary", "arbitrary"),
            vmem_limit_bytes=VMEM_LIMIT),
    )(x, g, w_in, wv_t, wf_t, b_f, conv_w)


def _gelu_tanh(x):
    c = math.sqrt(2.0 / math.pi)
    return 0.5 * x * (1.0 + jnp.tanh(c * (x + 0.044715 * (x * x * x))))


def _s5_kernel(u_ref, kern_ref, wre_ref, wim_ref, vre_ref, vim_ref, are_ref, aim_ref, d_ref,
               glu0_ref, y_ref, xre_ref, xim_ref, stre_ref, stim_ref, yt_ref, m_ref, glu_ref):
    j = pl.program_id(1)
    gb, bsz, n_chunks, width = u_ref.shape
    n_pairs = gb // 2
    rows = bsz * n_chunks

    @pl.when(j == 0)
    def _():
        stre_ref[...] = jnp.zeros_like(stre_ref)
        stim_ref[...] = jnp.zeros_like(stim_ref)
        lane = lax.broadcasted_iota(jnp.int32, (S5_RUN, width), 1)
        for g in range(gb):
            k_slab = kern_ref[g]
            w_slab = glu0_ref[g]
            for jj in range(S5_T):
                rows_j = slice(jj * S5_RUN, (jj + 1) * S5_RUN)
                k_j = pltpu.roll(k_slab, jj * S5_RUN, 1) if jj else k_slab
                w_j = pltpu.roll(w_slab, jj * S5_RUN, 1) if jj else w_slab
                m_ref[g, rows_j, :] = jnp.where(lane >= jj * S5_RUN, k_j, 0.0).astype(BF16)
                glu_ref[g, rows_j, :] = w_j.astype(BF16)

    def u_rows(g):
        return u_ref[g].reshape(rows, width)

    for pr in range(n_pairs):
        u_pair = jnp.concatenate([u_rows(2 * pr), u_rows(2 * pr + 1)], axis=1)
        xre_ref[pr] = jnp.dot(u_pair, wre_ref[pr], preferred_element_type=F32)
        xim_ref[pr] = jnp.dot(u_pair, wim_ref[pr], preferred_element_type=F32)

    def scan(pr):
        sr, si = stre_ref[pr], stim_ref[pr]
        ar = are_ref[:, pr * LANES:(pr + 1) * LANES]
        ai = aim_ref[:, pr * LANES:(pr + 1) * LANES]
        for c in range(n_chunks):
            r = pl.ds(c, bsz, stride=n_chunks)
            xr = xre_ref[pr, r, :]
            xi = xim_ref[pr, r, :]
            xre_ref[pr, r, :] = sr
            xim_ref[pr, r, :] = si
            sr, si = ar * sr - ai * si + xr, ar * si + ai * sr + xi
        stre_ref[pr], stim_ref[pr] = sr, si

    for g in range(gb):
        if g % 2 == 0:
            scan(g // 2)
        u = u_rows(g)
        y = jnp.dot(u, m_ref[g], preferred_element_type=F32)
        y = y + jnp.dot(xre_ref[g // 2].astype(BF16), vre_ref[g], preferred_element_type=F32)
        y = y + jnp.dot(xim_ref[g // 2].astype(BF16), vim_ref[g], preferred_element_type=F32)
        y = _gelu_tanh(y + d_ref[g] * u.astype(F32))
        gate = jnp.dot(y.astype(BF16), glu_ref[g], preferred_element_type=F32)
        y = y * jax.nn.sigmoid(gate)
        for t8 in range(width // LANES):
            yt_ref[g, t8] = y[:, t8 * LANES:(t8 + 1) * LANES]

    for b in range(bsz):
        for t8 in range(width // LANES):
            zs = [yt_ref[g, t8, b * n_chunks:(b + 1) * n_chunks, :] for g in range(gb)]
            for tt, rows_t in enumerate(_transpose_runs(zs)):
                y_ref[b, pl.ds(t8 * gb + tt, n_chunks, stride=S5_T), :] = rows_t


def _s5_params(a_re, a_im, log_dt, b_re, b_im, c_re, c_im, d_skip, w_glu):
    depth, g, p, h = b_re.shape
    t_sub = S5_T
    f = lambda v: v.astype(F32)
    tr = lambda v: jnp.swapaxes(f(v), -1, -2)
    lam_re = jnp.minimum(f(a_re), -1e-4)
    lam_im = f(a_im)
    dt = jnp.exp(f(log_dt))[..., None]
    rho, theta = lam_re * dt, lam_im * dt

    abar_re = jnp.exp(rho) * jnp.cos(theta)
    abar_im = jnp.exp(rho) * jnp.sin(theta)
    inv = 1.0 / (lam_re * lam_re + lam_im * lam_im)
    coef_re = (((abar_re - 1.0) * lam_re + abar_im * lam_im) * inv)[:, :, None, :]
    coef_im = ((abar_im * lam_re - (abar_re - 1.0) * lam_im) * inv)[:, :, None, :]
    bbar_re = coef_re * tr(b_re) - coef_im * tr(b_im)
    bbar_im = coef_re * tr(b_im) + coef_im * tr(b_re)

    tau = jnp.arange(t_sub + 1, dtype=F32)[:, None]
    pw_mag = jnp.exp(tau * rho[:, :, None, :])
    pw_re = pw_mag * jnp.cos(tau * theta[:, :, None, :])
    pw_im = pw_mag * jnp.sin(tau * theta[:, :, None, :])

    pr, pi = pw_re[:, :, :t_sub, None, :], pw_im[:, :, :t_sub, None, :]
    ab_re = pr * bbar_re[:, :, None] - pi * bbar_im[:, :, None]
    ab_im = pr * bbar_im[:, :, None] + pi * bbar_re[:, :, None]

    kern = (jnp.einsum('dgop,dgthp->dghto', f(c_re), ab_re)
            - jnp.einsum('dgop,dgthp->dghto', f(c_im), ab_im)).reshape(depth, g, h, t_sub * h)

    def pair_block_diag(w):
        w = jnp.flip(w, axis=2).reshape(depth, g // 2, 2, t_sub * h, p)
        return jnp.einsum('dqirc,ij->dqirjc', w, jnp.eye(2, dtype=F32)).reshape(
            depth, g // 2, 2 * t_sub * h, 2 * p).astype(BF16)

    w_re, w_im = pair_block_diag(ab_re), pair_block_diag(ab_im)

    p1_re, p1_im = tr(pw_re[:, :, 1:])[..., None], tr(pw_im[:, :, 1:])[..., None]
    ct_re, ct_im = tr(c_re)[:, :, :, None, :], tr(c_im)[:, :, :, None, :]

    def pair_rows(v):
        v = v.reshape(depth, g // 2, 2, p, t_sub * h)
        return jnp.einsum('dqipc,ij->dqijpc', v, jnp.eye(2, dtype=F32)).reshape(
            depth, g, 2 * p, t_sub * h).astype(BF16)

    v_re = pair_rows(ct_re * p1_re - ct_im * p1_im)
    v_im = pair_rows(-(ct_re * p1_im + ct_im * p1_re))

    at_re = jnp.broadcast_to(pw_re[:, :, t_sub].reshape(depth, 1, g * p), (depth, SUBLANES, g * p))
    at_im = jnp.broadcast_to(pw_im[:, :, t_sub].reshape(depth, 1, g * p), (depth, SUBLANES, g * p))

    d_t = jnp.tile(f(d_skip)[:, :, None, :], (1, 1, t_sub, 1)).reshape(depth, g, 1, t_sub * h)
    glu0 = jnp.pad(f(w_glu), ((0, 0), (0, 0), (0, 0), (0, (t_sub - 1) * h)))
    return kern, w_re, w_im, v_re, v_im, at_re, at_im, d_t, glu0


def _s5(layer, u_t, params):
    g, bsz, n_chunks, width = u_t.shape
    kern, w_re, w_im, v_re, v_im, at_re, at_im, d_t, glu0 = params
    gb = LANES // S5_RUN
    cb = S5_CBLK
    rows = bsz * cb
    n_pairs = gb // 2
    assert bsz == SUBLANES and width == 2 * LANES and S5_T == 2 * gb

    def per_group(arr, groups):
        blk = (None, groups) + arr.shape[2:]
        return pl.BlockSpec(blk, lambda i, j: (layer, i) + (0,) * (arr.ndim - 2))

    decay_spec = pl.BlockSpec((None, SUBLANES, n_pairs * LANES), lambda i, j: (layer, 0, i))
    return pl.pallas_call(
        _s5_kernel,
        name="s5",
        grid=(g // gb, n_chunks // cb),
        in_specs=[pl.BlockSpec((gb, bsz, cb, width), lambda i, j: (i, 0, j, 0)),
                  per_group(kern, gb), per_group(w_re, n_pairs), per_group(w_im, n_pairs),
                  per_group(v_re, gb), per_group(v_im, gb), decay_spec, decay_spec,
                  per_group(d_t, gb), per_group(glu0, gb)],
        out_specs=pl.BlockSpec((bsz, cb * S5_T, LANES), lambda i, j: (0, j, i)),
        out_shape=jax.ShapeDtypeStruct((bsz, n_chunks * S5_T, g * S5_RUN), F32),
        scratch_shapes=[
            pltpu.VMEM((n_pairs, rows, LANES), F32),
            pltpu.VMEM((n_pairs, rows, LANES), F32),
            pltpu.VMEM((n_pairs, SUBLANES, LANES), F32),
            pltpu.VMEM((n_pairs, SUBLANES, LANES), F32),
            pltpu.VMEM((gb, width // LANES, rows, LANES), F32),
            pltpu.VMEM((gb, width, width), BF16),
            pltpu.VMEM((gb, width, width), BF16),
        ],
        compiler_params=pltpu.CompilerParams(
            dimension_semantics=("arbitrary", "arbitrary"),
            vmem_limit_bytes=VMEM_LIMIT),
    )(u_t, kern, w_re, w_im, v_re, v_im, at_re, at_im, d_t, glu0)


def _attn_kernel(q_ref, k_ref, kc_ref, vt_ref, o_ref, s_ref, bmax_ref, m_ref, acc_ref, *,
                 n_q_blocks):
    pair = pl.program_id(1)
    qi = pl.program_id(2)
    tq = q_ref.shape[0]
    ts = TS_ATTN
    n_strips = tq // ts
    assert n_strips % 2 == 0
    q = q_ref[...]
    lane = lax.broadcasted_iota(jnp.int32, (ts, LANES), 1)
    k_pos = lax.broadcasted_iota(jnp.int32, (ts, ts), 0)
    q_pos = lax.broadcasted_iota(jnp.int32, (ts, ts), 1)
    nt_dims = (((1,), (1,)), ((), ()))

    chains = [(st, hh) for st in range(n_strips) for hh in range(2)]
    qs = {}
    for st, hh in chains:
        q_st = q[st * ts:(st + 1) * ts, :]
        head_lanes = (lane >= HEAD_DIM) if hh else (lane < HEAD_DIM)
        piece_lanes = (lane < CUM_PIECES * ATTN_HEADS) & (lane % ATTN_HEADS == 2 * pair + hh)
        qs[st, hh] = jnp.concatenate(
            [jnp.where(head_lanes, q_st, jnp.zeros_like(q_st)),
             jnp.where(piece_lanes, -1.0, 0.0).astype(BF16)], axis=1)

    def key_start(blk):
        return blk * ts

    def scores(blk, buf, first_strip=0):
        k0 = key_start(blk)
        kb = jnp.concatenate([k_ref[pl.ds(k0, ts), :], kc_ref[pl.ds(k0, ts), :]], axis=1)
        for c, (st, hh) in enumerate(chains):
            if st < first_strip:
                continue
            s = lax.dot_general(kb, qs[st, hh], nt_dims, preferred_element_type=F32)
            s_ref[buf, c] = s
            bmax_ref[buf, c] = jnp.max(s, axis=0, keepdims=True)

    def softmax_pv(blk, buf, first_strip=0, diag_strip=None):
        vt = vt_ref[:, pl.ds(key_start(blk), ts)]
        vt_ones = [jnp.concatenate([vt[hh * HEAD_DIM:(hh + 1) * HEAD_DIM, :], ones_rows], axis=0)
                   for hh in range(2)]
        for c, (st, hh) in enumerate(chains):
            if st < first_strip:
                continue
            s = s_ref[buf, c]
            blk_max = bmax_ref[buf, c]
            if st == diag_strip:
                s = jnp.where(k_pos <= q_pos, s, MASK_VALUE)
                blk_max = jnp.max(s, axis=0, keepdims=True)
            m = m_ref[c]
            m_new = jnp.maximum(m, blk_max)
            alpha = jnp.exp2(m - m_new)
            p = jnp.exp2(s - m_new).astype(BF16)
            pv = jnp.dot(vt_ones[hh], p, preferred_element_type=F32)
            acc_ref[c] = alpha * acc_ref[c] + pv
            m_ref[c] = m_new

    m_ref[...] = jnp.full(m_ref.shape, MASK_VALUE, F32)
    acc_ref[...] = jnp.zeros_like(acc_ref)
    ones_rows = jnp.ones((BF16_SUBLANES, ts), BF16)

    def run(n_full):
        scores(0, 0)
        for t in range(n_full // 2):
            scores(2 * t + 1, 1)
            softmax_pv(2 * t, 0)
            scores(2 * t + 2, 0)
            softmax_pv(2 * t + 1, 1)
        for d in range(n_strips):
            if d + 1 < n_strips:
                scores(n_full + d + 1, (d + 1) % 2, first_strip=d + 1)
            softmax_pv(n_full + d, d % 2, first_strip=d, diag_strip=d)

    for q_blk in range(n_q_blocks):
        pl.when(qi == q_blk)(lambda q_blk=q_blk: run(q_blk * n_strips))

    out_t = jnp.concatenate(
        [jnp.concatenate([acc_ref[c, 0:HEAD_DIM, :] / acc_ref[c, HEAD_DIM:HEAD_DIM + 1, :]
                          for c in range(st * 2, st * 2 + 2)], axis=0)
         for st in range(n_strips)], axis=1)
    o_ref[...] = out_t.T.astype(BF16)


def _attention(q, k, kc, vt):
    bsz, seq, w = q.shape
    pairs = w // LANES
    tq = T_ATTN
    ts = TS_ATTN
    n_chains = 2 * (tq // ts)
    return pl.pallas_call(
        functools.partial(_attn_kernel, n_q_blocks=seq // tq),
        name="fox_attn",
        grid=(bsz, pairs, seq // tq),
        in_specs=[
            pl.BlockSpec((None, tq, LANES), lambda b, p, i: (b, i, p)),
            pl.BlockSpec((None, seq, LANES), lambda b, p, i: (b, 0, p)),
            pl.BlockSpec((None, seq, LANES), lambda b, p, i: (b, 0, 0)),
            pl.BlockSpec((None, LANES, seq), lambda b, p, i: (b, p, 0)),
        ],
        out_specs=pl.BlockSpec((None, tq, LANES), lambda b, p, i: (b, i, p)),
        out_shape=jax.ShapeDtypeStruct(q.shape, BF16),
        scratch_shapes=[
            pltpu.VMEM((2, n_chains, ts, ts), F32),
            pltpu.VMEM((2, n_chains, 1, ts), F32),
            pltpu.VMEM((n_chains, 1, ts), F32),
            pltpu.VMEM((n_chains, HEAD_DIM + BF16_SUBLANES, ts), F32),
        ],
        compiler_params=pltpu.CompilerParams(
            dimension_semantics=("arbitrary", "arbitrary", "arbitrary"),
            vmem_limit_bytes=VMEM_LIMIT),
    )(q, k, kc, vt)


def _merge_kernel(x_ref, ya_ref, yb_ref, yc_ref, gpre_ref, wg_ref, bg_ref, wbr_ref, wout_ref,
                  gpost_ref, o_ref):
    x = x_ref[...]
    d = x.shape[1]
    hn = _rms(x, gpre_ref[...]).astype(BF16)
    merged = None
    for i, y_ref in enumerate((ya_ref, yb_ref, yc_ref)):
        cols = slice(i * d, (i + 1) * d)
        gate = jax.nn.sigmoid(
            jnp.dot(hn, wg_ref[:, cols], preferred_element_type=F32) + bg_ref[:, cols])
        br = jnp.dot(y_ref[...].astype(BF16), wbr_ref[i * BRANCH_WIDTH:(i + 1) * BRANCH_WIDTH, :],
                     preferred_element_type=F32)
        merged = gate * br if merged is None else merged + gate * br
    mix = jnp.dot(merged.astype(BF16), wout_ref[...], preferred_element_type=F32)
    o_ref[...] = x + _rms(mix, gpost_ref[...])


def _merge(layer, x, ya, yb, yc, g_pre, w_g, b_g, w_br, w_out, g_post):
    n, d = x.shape
    tm = TM_PROJ
    row_blk = lambda width: pl.BlockSpec((tm, width), lambda i: (i, 0))
    consts = (g_pre, w_g, b_g, w_br, w_out, g_post)
    return pl.pallas_call(
        _merge_kernel,
        name="merge",
        grid=(n // tm,),
        in_specs=[row_blk(d)] + [row_blk(BRANCH_WIDTH)] * 3 + [_layer_spec(c, layer) for c in consts],
        out_specs=row_blk(d),
        out_shape=jax.ShapeDtypeStruct(x.shape, F32),
        compiler_params=pltpu.CompilerParams(
            dimension_semantics=("arbitrary",),
            vmem_limit_bytes=VMEM_LIMIT),
    )(x, ya, yb, yc, *consts)


def _mlp_kernel(x_ref, gpre_ref, w1_ref, w2_ref, gpost_ref, o_ref):
    x = x_ref[...]
    d = x.shape[1]
    hn = _rms(x, gpre_ref[...]).astype(BF16)
    acc = None
    for c in range(w1_ref.shape[1] // d):
        cols = slice(c * d, (c + 1) * d)
        h1 = jnp.dot(hn, w1_ref[:, cols], preferred_element_type=F32)
        h1 = jnp.square(jnp.maximum(h1, 0.0)).astype(BF16)
        part = jnp.dot(h1, w2_ref[cols, :], preferred_element_type=F32)
        acc = part if acc is None else acc + part
    o_ref[...] = x + _rms(acc, gpost_ref[...])


def _mlp(layer, x, g_pre, w1, w2, g_post):
    n, d = x.shape
    tm = TM_PROJ
    blk = pl.BlockSpec((tm, d), lambda i: (i, 0))
    consts = (g_pre, w1, w2, g_post)
    return pl.pallas_call(
        _mlp_kernel,
        name="mlp",
        grid=(n // tm,),
        in_specs=[blk] + [_layer_spec(c, layer) for c in consts],
        out_specs=blk,
        out_shape=jax.ShapeDtypeStruct(x.shape, F32),
        compiler_params=pltpu.CompilerParams(
            dimension_semantics=("arbitrary",),
            vmem_limit_bytes=VMEM_LIMIT),
    )(x, *consts)


def kernel(x, g_pre_mix, w_in, b_gate, s5_a_re, s5_a_im, s5_log_dt, s5_b_re, s5_b_im, s5_c_re,
           s5_c_im, s5_d, s5_w_glu, conv_w, fox_b_f, w_branch, w_out, g_post_mix, g_pre_mlp,
           w_ff1, w_ff2, g_post_mlp):
    bsz, seq, d = x.shape
    depth = g_pre_mix.shape[0]
    n = bsz * seq

    n_act = 6 * BRANCH_WIDTH
    n_v = n_act + BRANCH_WIDTH
    w_in_b = w_in.astype(BF16)
    wv_t = jnp.swapaxes(w_in_b[:, :, n_act:n_v], 1, 2)
    wf_t = jnp.swapaxes(w_in_b[:, :, n_v:n_v + ATTN_HEADS], 1, 2)
    w_g = w_in_b[:, :, n_v + ATTN_HEADS:]
    w_br, w_o = w_branch.astype(BF16), w_out.astype(BF16)
    w1, w2 = w_ff1.astype(BF16), w_ff2.astype(BF16)
    row = lambda v: v.astype(F32).reshape(depth, 1, -1)
    g_mix, g_post, g_mlp, g_post2, b_g = map(
        row, (g_pre_mix, g_post_mix, g_pre_mlp, g_post_mlp, b_gate))
    b_f = fox_b_f.astype(F32).reshape(depth, ATTN_HEADS, 1)
    cw = conv_w.astype(F32)
    s5_params = _s5_params(s5_a_re, s5_a_im, s5_log_dt, s5_b_re, s5_b_im, s5_c_re, s5_c_im,
                           s5_d, s5_w_glu)

    flat = lambda t: t.reshape(n, t.shape[-1])
    for layer in range(depth):
        ut, yb, q, k, vt, kc = _in_proj(layer, x, g_mix, w_in_b, wv_t, wf_t, b_f, cw)
        ya = _s5(layer, ut, s5_params)
        yc = _attention(q, k, kc, vt)
        x1 = _merge(layer, flat(x), flat(ya), flat(yb), flat(yc), g_mix, w_g, b_g, w_br, w_o,
                    g_post)
        x = _mlp(layer, x1, g_mlp, w1, w2, g_post2).reshape(bsz, seq, d)
    return x
```

```python
import functools
import math

import jax
import jax.numpy as jnp
from jax import lax
from jax.experimental import pallas as pl
from jax.experimental.pallas import tpu as pltpu

F32 = jnp.float32
BF16 = jnp.bfloat16

EPS = 1e-6
MASK_VALUE = -1e30

ATTN_HEADS = 8
HEAD_DIM = 64
CONV_K = 3
BRANCH_WIDTH = 512

LANES = 128
SUBLANES = 8
BF16_SUBLANES = 16
VMEM_LIMIT = 56 * 1024 * 1024

TM_PROJ = 1024
S5_RUN = 16
S5_T = 16
S5_CBLK = 64
T_ATTN = 1024
TS_ATTN = 256
LOG2E = math.log2(math.e)
CUM_PIECES = 3


def _rms(x, g):
    return x * lax.rsqrt(jnp.mean(x * x, axis=-1, keepdims=True) + EPS) * g


def _transpose_runs(arrs):
    n = len(arrs)
    assert n * S5_RUN == LANES
    run = lax.broadcasted_iota(jnp.int32, arrs[0].shape, 1) // S5_RUN
    d = n // 2
    while d:
        low = (run & d) == 0
        nxt = list(arrs)
        for i in range(n):
            if not i & d:
                a, b = arrs[i], arrs[i + d]
                nxt[i] = jnp.where(low, a, pltpu.roll(b, S5_RUN * d, 1))
                nxt[i + d] = jnp.where(low, pltpu.roll(a, LANES - S5_RUN * d, 1), b)
        arrs = nxt
        d //= 2
    return arrs


def _cumsum_lanes(x):
    n = x.shape[-1]
    lane = lax.broadcasted_iota(jnp.int32, x.shape, x.ndim - 1)
    shift = 1
    while shift < n:
        x = x + jnp.where(lane >= shift, pltpu.roll(x, shift, x.ndim - 1), 0.0)
        shift *= 2
    return x


def _layer_spec(arr, layer, block=None):
    block = arr.shape[1:] if block is None else block
    return pl.BlockSpec((None,) + tuple(block), lambda *_: (layer,) + (0,) * len(block),
                        pipeline_mode=pl.Buffered(1))


def _inproj_kernel(x_ref, g_ref, w_ref, wvt_ref, wft_ref, bf_ref, cw_ref,
                   ut_ref, yb_ref, q_ref, k_ref, vt_ref, kc_ref,
                   vtail_ref, carry_ref, uscr_ref):
    j = pl.program_id(1)
    tm = x_ref.shape[0]
    w = BRANCH_WIDTH

    @pl.when(j == 0)
    def _():
        vtail_ref[...] = jnp.zeros_like(vtail_ref)
        carry_ref[...] = jnp.zeros_like(carry_ref)

    hn = _rms(x_ref[...], g_ref[...]).astype(BF16)

    def proj(c):
        return jnp.dot(hn, w_ref[:, c * w:(c + 1) * w], preferred_element_type=F32)


    nt_dims = (((1,), (1,)), ((), ()))
    fl = lax.dot_general(wft_ref[...], hn, nt_dims, preferred_element_type=F32)
    z = fl + bf_ref[...]
    log_f = (jnp.minimum(z, 0.0) - jnp.log1p(jnp.exp(-jnp.abs(z)))) * LOG2E
    cum = _cumsum_lanes(log_f) + carry_ref[:, 0:1]
    carry_ref[...] = jnp.broadcast_to(cum[:, tm - 1:tm], carry_ref.shape)
    hi = cum.astype(BF16).astype(F32)
    mid = (cum - hi).astype(BF16).astype(F32)
    lo = cum - hi - mid
    pieces = jnp.concatenate(
        [hi, mid, lo, jnp.zeros((LANES - CUM_PIECES * ATTN_HEADS, tm), F32)], axis=0)
    kc_ref[...] = pieces.T.astype(BF16)

    u = proj(0)
    n_lt = w // LANES
    for lt in range(n_lt):
        uscr_ref[lt] = u[:, lt * LANES:(lt + 1) * LANES]
    n_sub = tm // S5_T
    runs_per_tile = LANES // S5_RUN
    for lt in range(n_lt):
        for t8 in range(S5_T // runs_per_tile):
            xs = [uscr_ref[lt, pl.ds(t8 * runs_per_tile + tt, n_sub, stride=S5_T), :]
                  for tt in range(runs_per_tile)]
            for gi, flat in enumerate(_transpose_runs(xs)):
                ut_ref[lt * runs_per_tile + gi, :, t8 * LANES:(t8 + 1) * LANES] = flat.astype(BF16)

    vv = proj(3) * proj(1)
    tail = vtail_ref[...]
    row = lax.broadcasted_iota(jnp.int32, vv.shape, 0)
    v1 = jnp.where(row == 0, tail[7:8, :], pltpu.roll(vv, 1, 0))
    v2 = jnp.where(row == 0, tail[6:7, :],
                   jnp.where(row == 1, tail[7:8, :], pltpu.roll(vv, 2, 0)))
    cw = cw_ref[...]
    conv = v2 * cw[0:1, :] + v1 * cw[1:2, :] + vv * cw[2:3, :]
    yb_ref[...] = (proj(2) * conv).astype(BF16)
    vtail_ref[...] = vv[tm - SUBLANES:tm, :]

    q_ref[...] = (proj(4) * (HEAD_DIM ** -0.5 * LOG2E)).astype(BF16)
    vt_ref[...] = lax.dot_general(wvt_ref[...], hn, nt_dims,
                                  preferred_element_type=F32).astype(BF16)
    k_ref[...] = proj(5).astype(BF16)


def _in_proj(layer, x, g, w_in, wv_t, wf_t, b_f, conv_w):
    bsz, seq, d = x.shape
    tm = TM_PROJ
    n_act = 6 * BRANCH_WIDTH
    n_groups = BRANCH_WIDTH // S5_RUN
    act =jax.ShapeDtypeStruct((bsz, seq, BRANCH_WIDTH), BF16)
    act_spec = pl.BlockSpec((None, tm, BRANCH_WIDTH), lambda b, j: (b, j, 0))
    return pl.pallas_call(
        _inproj_kernel,
        name="in_proj",
        grid=(bsz, seq // tm),
        in_specs=[
            pl.BlockSpec((None, tm, d), lambda b, j: (b, j, 0)),
            _layer_spec(g, layer),
            _layer_spec(w_in, layer, (d, n_act)),
            _layer_spec(wv_t, layer),
            _layer_spec(wf_t, layer),
            _layer_spec(b_f, layer),
            _layer_spec(conv_w, layer),
        ],
        out_specs=[
            pl.BlockSpec((n_groups, None, tm // S5_T, S5_T * S5_RUN), lambda b, j: (0, b, j, 0))
        ] + [act_spec] * 3 + [
            pl.BlockSpec((None, BRANCH_WIDTH, tm), lambda b, j: (b, 0, j)),
            pl.BlockSpec((None, tm, LANES), lambda b, j: (b, j, 0))],
        out_shape=[
            jax.ShapeDtypeStruct((n_groups, bsz, seq // S5_T, S5_T * S5_RUN), BF16)
        ] + [act] * 3 + [
            jax.ShapeDtypeStruct((bsz, BRANCH_WIDTH, seq), BF16),
            jax.ShapeDtypeStruct((bsz, seq, LANES), BF16)],
        scratch_shapes=[pltpu.VMEM((SUBLANES, BRANCH_WIDTH), F32),
                        pltpu.VMEM((ATTN_HEADS, LANES), F32),
                        pltpu.VMEM((BRANCH_WIDTH // LANES, tm, LANES), F32)],
        compiler_params=pltpu.CompilerParams(
            dimension_semantics=("arbitrary", "arbitrary"),
            vmem_limit_bytes=VMEM_LIMIT),
    )(x, g, w_in, wv_t, wf_t, b_f, conv_w)


def _gelu_tanh(x):
    c = math.sqrt(2.0 / math.pi)
    return 0.5 * x * (1.0 + jnp.tanh(c * (x + 0.044715 * (x * x * x))))


def _s5_kernel(u_ref, kern_ref, wre_ref, wim_ref, vre_ref, vim_ref, are_ref, aim_ref, d_ref,
               glu0_ref, y_ref, xre_ref, xim_ref, stre_ref, stim_ref, yt_ref, m_ref, glu_ref):
    j = pl.program_id(1)
    gb, bsz, n_chunks, width = u_ref.shape
    n_pairs = gb // 2
    rows = bsz * n_chunks

    @pl.when(j == 0)
    def _():
        stre_ref[...] = jnp.zeros_like(stre_ref)
        stim_ref[...] = jnp.zeros_like(stim_ref)
        lane = lax.broadcasted_iota(jnp.int32, (S5_RUN, width), 1)
        for g in range(gb):
            k_slab = kern_ref[g]
            w_slab = glu0_ref[g]
            for jj in range(S5_T):
                rows_j = slice(jj * S5_RUN, (jj + 1) * S5_RUN)
                k_j = pltpu.roll(k_slab, jj * S5_RUN, 1) if jj else k_slab
                w_j = pltpu.roll(w_slab, jj * S5_RUN, 1) if jj else w_slab
                m_ref[g, rows_j, :] = jnp.where(lane >= jj * S5_RUN, k_j, 0.0).astype(BF16)
                glu_ref[g, rows_j, :] = w_j.astype(BF16)

    def u_rows(g):
        return u_ref[g].reshape(rows, width)

    for pr in range(n_pairs):
        u_pair = jnp.concatenate([u_rows(2 * pr), u_rows(2 * pr + 1)], axis=1)
        xre_ref[pr] = jnp.dot(u_pair, wre_ref[pr], preferred_element_type=F32)
        xim_ref[pr] = jnp.dot(u_pair, wim_ref[pr], preferred_element_type=F32)

    def scan(pr):
        sr, si = stre_ref[pr], stim_ref[pr]
        ar = are_ref[:, pr * LANES:(pr + 1) * LANES]
        ai = aim_ref[:, pr * LANES:(pr + 1) * LANES]
        for c in range(n_chunks):
            r = pl.ds(c, bsz, stride=n_chunks)
            xr = xre_ref[pr, r, :]
            xi = xim_ref[pr, r, :]
            xre_ref[pr, r, :] = sr
            xim_ref[pr, r, :] = si
            sr, si = ar * sr - ai * si + xr, ar * si + ai * sr + xi
        stre_ref[pr], stim_ref[pr] = sr, si

    for g in range(gb):
        if g % 2 == 0:
            scan(g // 2)
        u = u_rows(g)
        y = jnp.dot(u, m_ref[g], preferred_element_type=F32)
        y = y + jnp.dot(xre_ref[g // 2].astype(BF16), vre_ref[g], preferred_element_type=F32)
        y = y + jnp.dot(xim_ref[g // 2].astype(BF16), vim_ref[g], preferred_element_type=F32)
        y = _gelu_tanh(y + d_ref[g] * u.astype(F32))
        gate = jnp.dot(y.astype(BF16), glu_ref[g], preferred_element_type=F32)
        y = y * jax.nn.sigmoid(gate)
        for t8 in range(width // LANES):
            yt_ref[g, t8] = y[:, t8 * LANES:(t8 + 1) * LANES]

    for b in range(bsz):
        for t8 in range(width // LANES):
            zs = [yt_ref[g, t8, b * n_chunks:(b + 1) * n_chunks, :] for g in range(gb)]
            for tt, rows_t in enumerate(_transpose_runs(zs)):
                y_ref[b, pl.ds(t8 * gb + tt, n_chunks, stride=S5_T), :] = rows_t


def _s5_params(a_re, a_im, log_dt, b_re, b_im, c_re, c_im, d_skip, w_glu):
    depth, g, p, h = b_re.shape
    t_sub = S5_T
    f = lambda v: v.astype(F32)
    tr = lambda v: jnp.swapaxes(f(v), -1, -2)
    lam_re = jnp.minimum(f(a_re), -1e-4)
    lam_im = f(a_im)
    dt = jnp.exp(f(log_dt))[..., None]
    rho, theta = lam_re * dt, lam_im * dt

    abar_re = jnp.exp(rho) * jnp.cos(theta)
    abar_im = jnp.exp(rho) * jnp.sin(theta)
    inv = 1.0 / (lam_re * lam_re + lam_im * lam_im)
    coef_re = (((abar_re - 1.0) * lam_re + abar_im * lam_im) * inv)[:, :, None, :]
    coef_im = ((abar_im * lam_re - (abar_re - 1.0) * lam_im) * inv)[:, :, None, :]
    bbar_re = coef_re * tr(b_re) - coef_im * tr(b_im)
    bbar_im = coef_re * tr(b_im) + coef_im * tr(b_re)

    tau = jnp.arange(t_sub + 1, dtype=F32)[:, None]
    pw_mag = jnp.exp(tau * rho[:, :, None, :])
    pw_re = pw_mag * jnp.cos(tau * theta[:, :, None, :])
    pw_im = pw_mag * jnp.sin(tau * theta[:, :, None, :])

    pr, pi = pw_re[:, :, :t_sub, None, :], pw_im[:, :, :t_sub, None, :]
    ab_re = pr * bbar_re[:, :, None] - pi * bbar_im[:, :, None]
    ab_im = pr * bbar_im[:, :, None] + pi * bbar_re[:, :, None]

    kern = (jnp.einsum('dgop,dgthp->dghto', f(c_re), ab_re)
            - jnp.einsum('dgop,dgthp->dghto', f(c_im), ab_im)).reshape(depth, g, h, t_sub * h)

    def pair_block_diag(w):
        w = jnp.flip(w, axis=2).reshape(depth, g // 2, 2, t_sub * h, p)
        return jnp.einsum('dqirc,ij->dqirjc', w, jnp.eye(2, dtype=F32)).reshape(
            depth, g // 2, 2 * t_sub * h, 2 * p).astype(BF16)

    w_re, w_im = pair_block_diag(ab_re), pair_block_diag(ab_im)

    p1_re, p1_im = tr(pw_re[:, :, 1:])[..., None], tr(pw_im[:, :, 1:])[..., None]
    ct_re, ct_im = tr(c_re)[:, :, :, None, :], tr(c_im)[:, :, :, None, :]

    def pair_rows(v):
        v = v.reshape(depth, g // 2, 2, p, t_sub * h)
        return jnp.einsum('dqipc,ij->dqijpc', v, jnp.eye(2, dtype=F32)).reshape(
            depth, g, 2 * p, t_sub * h).astype(BF16)

    v_re = pair_rows(ct_re * p1_re - ct_im * p1_im)
    v_im = pair_rows(-(ct_re * p1_im + ct_im * p1_re))

    at_re = jnp.broadcast_to(pw_re[:, :, t_sub].reshape(depth, 1, g * p), (depth, SUBLANES, g * p))
    at_im = jnp.broadcast_to(pw_im[:, :, t_sub].reshape(depth, 1, g * p), (depth, SUBLANES, g * p))

    d_t = jnp.tile(f(d_skip)[:, :, None, :], (1, 1, t_sub, 1)).reshape(depth, g, 1, t_sub * h)
    glu0 = jnp.pad(f(w_glu), ((0, 0), (0, 0), (0, 0), (0, (t_sub - 1) * h)))
    return kern, w_re, w_im, v_re, v_im, at_re, at_im, d_t, glu0


def _s5(layer, u_t, params):
    g, bsz, n_chunks, width = u_t.shape
    kern, w_re, w_im, v_re, v_im, at_re, at_im, d_t, glu0 = params
    gb = LANES // S5_RUN
    cb = S5_CBLK
    rows = bsz * cb
    n_pairs = gb // 2
    assert bsz == SUBLANES and width == 2 * LANES and S5_T == 2 * gb

    def per_group(arr, groups):
        blk = (None, groups) + arr.shape[2:]
        return pl.BlockSpec(blk, lambda i, j: (layer, i) + (0,) * (arr.ndim - 2))

    decay_spec = pl.BlockSpec((None, SUBLANES, n_pairs * LANES), lambda i, j: (layer, 0, i))
    return pl.pallas_call(
        _s5_kernel,
        name="s5",
        grid=(g // gb, n_chunks // cb),
        in_specs=[pl.BlockSpec((gb, bsz, cb, width), lambda i, j: (i, 0, j, 0)),
                  per_group(kern, gb), per_group(w_re, n_pairs), per_group(w_im, n_pairs),
                  per_group(v_re, gb), per_group(v_im, gb), decay_spec, decay_spec,
                  per_group(d_t, gb), per_group(glu0, gb)],
        out_specs=pl.BlockSpec((bsz, cb * S5_T, LANES), lambda i, j: (0, j, i)),
        out_shape=jax.ShapeDtypeStruct((bsz, n_chunks * S5_T, g * S5_RUN), F32),
        scratch_shapes=[
            pltpu.VMEM((n_pairs, rows, LANES), F32),
            pltpu.VMEM((n_pairs, rows, LANES), F32),
            pltpu.VMEM((n_pairs, SUBLANES, LANES), F32),
            pltpu.VMEM((n_pairs, SUBLANES, LANES), F32),
            pltpu.VMEM((gb, width // LANES, rows, LANES), F32),
            pltpu.VMEM((gb, width, width), BF16),
            pltpu.VMEM((gb, width, width), BF16),
        ],
        compiler_params=pltpu.CompilerParams(
            dimension_semantics=("arbitrary", "arbitrary"),
            vmem_limit_bytes=VMEM_LIMIT),
    )(u_t, kern, w_re, w_im, v_re, v_im, at_re, at_im, d_t, glu0)


def _attn_kernel(q_ref, k_ref, kc_ref, vt_ref, o_ref, s_ref, bmax_ref, m_ref, acc_ref, *,
                 n_q_blocks):
    pair = pl.program_id(1)
    qi = pl.program_id(2)
    tq = q_ref.shape[0]
    ts = TS_ATTN
    n_strips = tq // ts
    assert n_strips % 2 == 0
    q = q_ref[...]
    lane = lax.broadcasted_iota(jnp.int32, (ts, LANES), 1)
    k_pos = lax.broadcasted_iota(jnp.int32, (ts, ts), 0)
    q_pos = lax.broadcasted_iota(jnp.int32, (ts, ts), 1)
    nt_dims = (((1,), (1,)), ((), ()))

    chains = [(st, hh) for st in range(n_strips) for hh in range(2)]
    qs = {}
    for st, hh in chains:
        q_st = q[st * ts:(st + 1) * ts, :]
        head_lanes = (lane >= HEAD_DIM) if hh else (lane < HEAD_DIM)
        piece_lanes = (lane < CUM_PIECES * ATTN_HEADS) & (lane % ATTN_HEADS == 2 * pair + hh)
        qs[st, hh] = jnp.concatenate(
            [jnp.where(head_lanes, q_st, jnp.zeros_like(q_st)),
             jnp.where(piece_lanes, -1.0, 0.0).astype(BF16)], axis=1)

    def key_start(blk):
        return blk * ts

    def scores(blk, buf, first_strip=0):
        k0 = key_start(blk)
        kb = jnp.concatenate([k_ref[pl.ds(k0, ts), :], kc_ref[pl.ds(k0, ts), :]], axis=1)
        for c, (st, hh) in enumerate(chains):
            if st < first_strip:
                continue
            s = lax.dot_general(kb, qs[st, hh], nt_dims, preferred_element_type=F32)
            s_ref[buf, c] = s
            bmax_ref[buf, c] = jnp.max(s, axis=0, keepdims=True)

    def softmax_pv(blk, buf, first_strip=0, diag_strip=None):
        vt = vt_ref[:, pl.ds(key_start(blk), ts)]
        vt_ones = [jnp.concatenate([vt[hh * HEAD_DIM:(hh + 1) * HEAD_DIM, :], ones_rows], axis=0)
                   for hh in range(2)]
        for c, (st, hh) in enumerate(chains):
            if st < first_strip:
                continue
            s = s_ref[buf, c]
            blk_max = bmax_ref[buf, c]
            if st == diag_strip:
                s = jnp.where(k_pos <= q_pos, s, MASK_VALUE)
                blk_max = jnp.max(s, axis=0, keepdims=True)
            m = m_ref[c]
            m_new = jnp.maximum(m, blk_max)
            alpha = jnp.exp2(m - m_new)
            p = jnp.exp2(s - m_new).astype(BF16)
            pv = jnp.dot(vt_ones[hh], p, preferred_element_type=F32)
            acc_ref[c] = alpha * acc_ref[c] + pv
            m_ref[c] = m_new

    m_ref[...] = jnp.full(m_ref.shape, MASK_VALUE, F32)
    acc_ref[...] = jnp.zeros_like(acc_ref)
    ones_rows = jnp.ones((BF16_SUBLANES, ts), BF16)

    def run(n_full):
        scores(0, 0)
        for t in range(n_full // 2):
            scores(2 * t + 1, 1)
            softmax_pv(2 * t, 0)
            scores(2 * t + 2, 0)
            softmax_pv(2 * t + 1, 1)
        for d in range(n_strips):
            if d + 1 < n_strips:
                scores(n_full + d + 1, (d + 1) % 2, first_strip=d + 1)
            softmax_pv(n_full + d, d % 2, first_strip=d, diag_strip=d)

    for q_blk in range(n_q_blocks):
        pl.when(qi == q_blk)(lambda q_blk=q_blk: run(q_blk * n_strips))

    out_t = jnp.concatenate(
        [jnp.concatenate([acc_ref[c, 0:HEAD_DIM, :] / acc_ref[c, HEAD_DIM:HEAD_DIM + 1, :]
                          for c in range(st * 2, st * 2 + 2)], axis=0)
         for st in range(n_strips)], axis=1)
    o_ref[...] = out_t.T.astype(BF16)


def _attention(q, k, kc, vt):
    bsz, seq, w = q.shape
    pairs = w // LANES
    tq = T_ATTN
    ts = TS_ATTN
    n_chains = 2 * (tq // ts)
    return pl.pallas_call(
        functools.partial(_attn_kernel, n_q_blocks=seq // tq),
        name="fox_attn",
        grid=(bsz, pairs, seq // tq),
        in_specs=[
            pl.BlockSpec((None, tq, LANES), lambda b, p, i: (b, i, p)),
            pl.BlockSpec((None, seq, LANES), lambda b, p, i: (b, 0, p)),
            pl.BlockSpec((None, seq, LANES), lambda b, p, i: (b, 0, 0)),
            pl.BlockSpec((None, LANES, seq), lambda b, p, i: (b, p, 0)),
        ],
        out_specs=pl.BlockSpec((None, tq, LANES), lambda b, p, i: (b, i, p)),
        out_shape=jax.ShapeDtypeStruct(q.shape, BF16),
        scratch_shapes=[
            pltpu.VMEM((2, n_chains, ts, ts), F32),
            pltpu.VMEM((2, n_chains, 1, ts), F32),
            pltpu.VMEM((n_chains, 1, ts), F32),
            pltpu.VMEM((n_chains, HEAD_DIM + BF16_SUBLANES, ts), F32),
        ],
        compiler_params=pltpu.CompilerParams(
            dimension_semantics=("arbitrary", "arbitrary", "arbitrary"),
            vmem_limit_bytes=VMEM_LIMIT),
    )(q, k, kc, vt)


def _gate_window(first_col, n_cols, total_cols):
    for width in range(-(-n_cols // LANES) * LANES, total_cols + 1, LANES):
        start = first_col // width * width
        if start + width <= total_cols and first_col + n_cols - (start + width) <= LANES:
            return start, width
    raise ValueError("no aligned gate-weight window")


def _merge_kernel(x_ref, ya_ref, yb_ref, yc_ref, gpre_ref, win_ref, wtail_ref, bg_ref, wbr_ref,
                  wout_ref, gpost_ref, o_ref, wg_ref, *, gate_offset):
    @pl.when(pl.program_id(0) == 0)
    def _():
        tile_off, lane_off = divmod(gate_offset, LANES)
        n_src = win_ref.shape[1] // LANES
        lane = lax.broadcasted_iota(jnp.int32, (win_ref.shape[0], LANES), 1)

        def shifted(t):
            src = wtail_ref[...] if t == n_src else win_ref[:, t * LANES:(t + 1) * LANES]
            return pltpu.roll(src.astype(F32), LANES - lane_off, 1)

        nxt = shifted(tile_off)
        for t in range(wg_ref.shape[1] // LANES):
            cur, nxt = nxt, shifted(tile_off + t + 1)
            wg_ref[:, t * LANES:(t + 1) * LANES] = jnp.where(
                lane < LANES - lane_off, cur, nxt).astype(BF16)

    x = x_ref[...]
    d = x.shape[1]
    hn = _rms(x, gpre_ref[...]).astype(BF16)
    merged = None
    for i, y_ref in enumerate((ya_ref, yb_ref, yc_ref)):
        cols = slice(i * d, (i + 1) * d)
        gate = jax.nn.sigmoid(
            jnp.dot(hn, wg_ref[:, cols], preferred_element_type=F32) + bg_ref[:, cols])
        br = jnp.dot(y_ref[...].astype(BF16), wbr_ref[i * BRANCH_WIDTH:(i + 1) * BRANCH_WIDTH, :],
                     preferred_element_type=F32)
        merged = gate * br if merged is None else merged + gate * br
    mix = jnp.dot(merged.astype(BF16), wout_ref[...], preferred_element_type=F32)
    o_ref[...] = x + _rms(mix, gpost_ref[...])


def _merge(layer, x, ya, yb, yc, g_pre, w_in, gate_col0, b_g, w_br, w_out, g_post):
    n, d = x.shape
    tm = TM_PROJ
    n_gate = b_g.shape[-1]
    total = w_in.shape[-1]
    start, width = _gate_window(gate_col0, n_gate, total)
    w_tail = jnp.pad(w_in[:, :, start + width:], ((0, 0), (0, 0), (0, LANES - (total - start - width))))
    row_blk = lambda width: pl.BlockSpec((tm, width), lambda i: (i, 0))
    win_spec = pl.BlockSpec((None, d, width), lambda i: (layer, 0, start // width),
                            pipeline_mode=pl.Buffered(1))
    consts = (w_tail, b_g, w_br, w_out, g_post)
    return pl.pallas_call(
        functools.partial(_merge_kernel, gate_offset=gate_col0 - start),
        name="merge",
        grid=(n // tm,),
        in_specs=[row_blk(d)] + [row_blk(BRANCH_WIDTH)] * 3 + [_layer_spec(g_pre, layer), win_spec]
        + [_layer_spec(c, layer) for c in consts],
        out_specs=row_blk(d),
        out_shape=jax.ShapeDtypeStruct(x.shape, F32),
        scratch_shapes=[pltpu.VMEM((d, n_gate), BF16)],
        compiler_params=pltpu.CompilerParams(
            dimension_semantics=("arbitrary",),
            vmem_limit_bytes=VMEM_LIMIT),
    )(x, ya, yb, yc, g_pre, w_in, *consts)


def _mlp_kernel(x_ref, gpre_ref, w1_ref, w2_ref, gpost_ref, o_ref):
    x = x_ref[...]
    d = x.shape[1]
    hn = _rms(x, gpre_ref[...]).astype(BF16)
    acc = None
    for c in range(w1_ref.shape[1] // d):
        cols = slice(c * d, (c + 1) * d)
        h1 = jnp.dot(hn, w1_ref[:, cols], preferred_element_type=F32)
        h1 = jnp.square(jnp.maximum(h1, 0.0)).astype(BF16)
        part = jnp.dot(h1, w2_ref[cols, :], preferred_element_type=F32)
        acc = part if acc is None else acc + part
    o_ref[...] = x + _rms(acc, gpost_ref[...])


def _mlp(layer, x, g_pre, w1, w2, g_post):
    n, d = x.shape
    tm = TM_PROJ
    blk = pl.BlockSpec((tm, d), lambda i: (i, 0))
    consts = (g_pre, w1, w2, g_post)
    return pl.pallas_call(
        _mlp_kernel,
        name="mlp",
        grid=(n // tm,),
        in_specs=[blk] + [_layer_spec(c, layer) for c in consts],
        out_specs=blk,
        out_shape=jax.ShapeDtypeStruct(x.shape, F32),
        compiler_params=pltpu.CompilerParams(
            dimension_semantics=("arbitrary",),
            vmem_limit_bytes=VMEM_LIMIT),
    )(x, *consts)


def kernel(x, g_pre_mix, w_in, b_gate, s5_a_re, s5_a_im, s5_log_dt, s5_b_re, s5_b_im, s5_c_re,
           s5_c_im, s5_d, s5_w_glu, conv_w, fox_b_f, w_branch, w_out, g_post_mix, g_pre_mlp,
           w_ff1, w_ff2, g_post_mlp):
    bsz, seq, d = x.shape
    depth = g_pre_mix.shape[0]
    n = bsz * seq

    n_act = 6 * BRANCH_WIDTH
    n_v = n_act + BRANCH_WIDTH
    w_in_b = w_in.astype(BF16)
    wv_t = jnp.swapaxes(w_in_b[:, :, n_act:n_v], 1, 2)
    wf_t = jnp.swapaxes(w_in_b[:, :, n_v:n_v + ATTN_HEADS], 1, 2)
    n_gate0 = n_v + ATTN_HEADS
    w_br, w_o = w_branch.astype(BF16), w_out.astype(BF16)
    w1, w2 = w_ff1.astype(BF16), w_ff2.astype(BF16)
    row = lambda v: v.astype(F32).reshape(depth, 1, -1)
    g_mix, g_post, g_mlp, g_post2, b_g = map(
        row, (g_pre_mix, g_post_mix, g_pre_mlp, g_post_mlp, b_gate))
    b_f = fox_b_f.astype(F32).reshape(depth, ATTN_HEADS, 1)
    cw = conv_w.astype(F32)
    s5_params = _s5_params(s5_a_re, s5_a_im, s5_log_dt, s5_b_re, s5_b_im, s5_c_re, s5_c_im,
                           s5_d, s5_w_glu)

    flat = lambda t: t.reshape(n, t.shape[-1])
    for layer in range(depth):
        ut, yb, q, k, vt, kc = _in_proj(layer, x, g_mix, w_in_b, wv_t, wf_t, b_f, cw)
        ya = _s5(layer, ut, s5_params)
        yc = _attention(q, k, kc, vt)
        x1 = _merge(layer, flat(x), flat(ya), flat(yb), flat(yc), g_mix, w_in_b, n_gate0, b_g,
                    w_br, w_o, g_post)
        x = _mlp(layer, x1, g_mlp, w1, w2, g_post2).reshape(bsz, seq, d)
    return x
```

```python
import functools
import math

import jax
import jax.numpy as jnp
from jax import lax
from jax.experimental import pallas as pl
from jax.experimental.pallas import tpu as pltpu

F32 = jnp.float32
BF16 = jnp.bfloat16

EPS = 1e-6
MASK_VALUE = -1e30

ATTN_HEADS = 8
HEAD_DIM = 64
CONV_K = 3
BRANCH_WIDTH = 512

LANES = 128
SUBLANES = 8
BF16_SUBLANES = 16
VMEM_LIMIT = 56 * 1024 * 1024

TM_PROJ = 1024
S5_RUN = 16
S5_T = 16
S5_CBLK = 64
T_ATTN = 1024
TS_ATTN = 256
LOG2E = math.log2(math.e)
CUM_PIECES = 3


def _rms(x, g):
    return x * lax.rsqrt(jnp.mean(x * x, axis=-1, keepdims=True) + EPS) * g


def _transpose_runs(arrs):
    n = len(arrs)
    assert n * S5_RUN == LANES
    run = lax.broadcasted_iota(jnp.int32, arrs[0].shape, 1) // S5_RUN
    d = n // 2
    while d:
        low = (run & d) == 0
        nxt = list(arrs)
        for i in range(n):
            if not i & d:
                a, b = arrs[i], arrs[i + d]
                nxt[i] = jnp.where(low, a, pltpu.roll(b, S5_RUN * d, 1))
                nxt[i + d] = jnp.where(low, pltpu.roll(a, LANES - S5_RUN * d, 1), b)
        arrs = nxt
        d //= 2
    return arrs


def _cumsum_lanes(x):
    n = x.shape[-1]
    lane = lax.broadcasted_iota(jnp.int32, x.shape, x.ndim - 1)
    shift = 1
    while shift < n:
        x = x + jnp.where(lane >= shift, pltpu.roll(x, shift, x.ndim - 1), 0.0)
        shift *= 2
    return x


def _layer_spec(arr, layer, block=None):
    block = arr.shape[1:] if block is None else block
    return pl.BlockSpec((None,) + tuple(block), lambda *_: (layer,) + (0,) * len(block),
                        pipeline_mode=pl.Buffered(1))


def _inproj_kernel(x_ref, g_ref, w_ref, wvt_ref, wft_ref, bf_ref, cw_ref,
                   ut_ref, yb_ref, q_ref, k_ref, vt_ref, kc_ref,
                   vtail_ref, carry_ref, uscr_ref):
    j = pl.program_id(1)
    tm = x_ref.shape[0]
    w = BRANCH_WIDTH

    @pl.when(j == 0)
    def _():
        vtail_ref[...] = jnp.zeros_like(vtail_ref)
        carry_ref[...] = jnp.zeros_like(carry_ref)

    hn = _rms(x_ref[...], g_ref[...]).astype(BF16)

    def proj(c):
        return jnp.dot(hn, w_ref[:, c * w:(c + 1) * w], preferred_element_type=F32)


    nt_dims = (((1,), (1,)), ((), ()))
    fl = lax.dot_general(wft_ref[...], hn, nt_dims, preferred_element_type=F32)
    z = fl + bf_ref[...]
    log_f = (jnp.minimum(z, 0.0) - jnp.log1p(jnp.exp(-jnp.abs(z)))) * LOG2E
    cum = _cumsum_lanes(log_f) + carry_ref[:, 0:1]
    carry_ref[...] = jnp.broadcast_to(cum[:, tm - 1:tm], carry_ref.shape)
    hi = cum.astype(BF16).astype(F32)
    mid = (cum - hi).astype(BF16).astype(F32)
    lo = cum - hi - mid
    pieces = jnp.concatenate(
        [hi, mid, lo, jnp.zeros((LANES - CUM_PIECES * ATTN_HEADS, tm), F32)], axis=0)
    kc_ref[...] = pieces.T.astype(BF16)

    u = proj(0)
    n_lt = w // LANES
    for lt in range(n_lt):
        uscr_ref[lt] = u[:, lt * LANES:(lt + 1) * LANES]
    n_sub = tm // S5_T
    runs_per_tile = LANES // S5_RUN
    for lt in range(n_lt):
        for t8 in range(S5_T // runs_per_tile):
            xs = [uscr_ref[lt, pl.ds(t8 * runs_per_tile + tt, n_sub, stride=S5_T), :]
                  for tt in range(runs_per_tile)]
            for gi, flat in enumerate(_transpose_runs(xs)):
                ut_ref[lt * runs_per_tile + gi, :, t8 * LANES:(t8 + 1) * LANES] = flat.astype(BF16)

    vv = proj(3) * proj(1)
    tail = vtail_ref[...]
    row = lax.broadcasted_iota(jnp.int32, vv.shape, 0)
    v1 = jnp.where(row == 0, tail[7:8, :], pltpu.roll(vv, 1, 0))
    v2 = jnp.where(row == 0, tail[6:7, :],
                   jnp.where(row == 1, tail[7:8, :], pltpu.roll(vv, 2, 0)))
    cw = cw_ref[...]
    conv = v2 * cw[0:1, :] + v1 * cw[1:2, :] + vv * cw[2:3, :]
    yb_ref[...] = (proj(2) * conv).astype(BF16)
    vtail_ref[...] = vv[tm - SUBLANES:tm, :]

    q_ref[...] = (proj(4) * (HEAD_DIM ** -0.5 * LOG2E)).astype(BF16)
    vt_ref[...] = lax.dot_general(wvt_ref[...], hn, nt_dims,
                                  preferred_element_type=F32).astype(BF16)
    k_ref[...] = proj(5).astype(BF16)


def _in_proj(layer, x, g, w_in, wv_t, wf_t, b_f, conv_w):
    bsz, seq, d = x.shape
    tm = TM_PROJ
    n_act = 6 * BRANCH_WIDTH
    n_groups = BRANCH_WIDTH // S5_RUN
    act = jax.ShapeDtypeStruct((bsz, seq, BRANCH_WIDTH), BF16)
    act_spec = pl.BlockSpec((None, tm, BRANCH_WIDTH), lambda b, j: (b, j, 0))
    return pl.pallas_call(
        _inproj_kernel,
        name="in_proj",
        grid=(bsz, seq // tm),
        in_specs=[
            pl.BlockSpec((None, tm, d), lambda b, j: (b, j, 0)),
            _layer_spec(g, layer),
            _layer_spec(w_in, layer, (d, n_act)),
            _layer_spec(wv_t, layer),
            _layer_spec(wf_t, layer),
            _layer_spec(b_f, layer),
            _layer_spec(conv_w, layer),
        ],
        out_specs=[
            pl.BlockSpec((n_groups, None, tm // S5_T, S5_T * S5_RUN), lambda b, j: (0, b, j, 0))
        ] + [act_spec] * 3 + [
            pl.BlockSpec((None, BRANCH_WIDTH, tm), lambda b, j: (b, 0, j)),
            pl.BlockSpec((None, tm, LANES), lambda b, j: (b, j, 0))],
        out_shape=[
            jax.ShapeDtypeStruct((n_groups, bsz, seq // S5_T, S5_T * S5_RUN), BF16)
        ] + [act] * 3 + [
            jax.ShapeDtypeStruct((bsz, BRANCH_WIDTH, seq), BF16),
            jax.ShapeDtypeStruct((bsz, seq, LANES), BF16)],
        scratch_shapes=[pltpu.VMEM((SUBLANES, BRANCH_WIDTH), F32),
                        pltpu.VMEM((ATTN_HEADS, LANES), F32),
                        pltpu.VMEM((BRANCH_WIDTH // LANES, tm, LANES), F32)],
        compiler_params=pltpu.CompilerParams(
            dimension_semantics=("arbitrary", "arbitrary"),
            vmem_limit_bytes=VMEM_LIMIT),
    )(x, g, w_in, wv_t, wf_t, b_f, conv_w)


def _gelu_tanh(x):
    c = math.sqrt(2.0 / math.pi)
    return 0.5 * x * (1.0 + jnp.tanh(c * (x + 0.044715 * (x * x * x))))


def _s5_kernel(u_ref, kern_ref, wre_ref, wim_ref, vre_ref, vim_ref, are_ref, aim_ref, d_ref,
               glu0_ref, y_ref, xre_ref, xim_ref, stre_ref, stim_ref, yt_ref, m_ref, glu_ref):
    j = pl.program_id(1)
    gb, bsz, n_chunks, width = u_ref.shape
    n_pairs = gb // 2
    rows = bsz * n_chunks

    @pl.when(j == 0)
    def _():
        stre_ref[...] = jnp.zeros_like(stre_ref)
        stim_ref[...] = jnp.zeros_like(stim_ref)
        lane = lax.broadcasted_iota(jnp.int32, (S5_RUN, width), 1)
        for g in range(gb):
            k_slab = kern_ref[g]
            w_slab = glu0_ref[g]
            for jj in range(S5_T):
                rows_j = slice(jj * S5_RUN, (jj + 1) * S5_RUN)
                k_j = pltpu.roll(k_slab, jj * S5_RUN, 1) if jj else k_slab
                w_j = pltpu.roll(w_slab, jj * S5_RUN, 1) if jj else w_slab
                m_ref[g, rows_j, :] = jnp.where(lane >= jj * S5_RUN, k_j, 0.0).astype(BF16)
                glu_ref[g, rows_j, :] = w_j.astype(BF16)

    def u_rows(g):
        return u_ref[g].reshape(rows, width)

    for pr in range(n_pairs):
        u_pair = jnp.concatenate([u_rows(2 * pr), u_rows(2 * pr + 1)], axis=1)
        xre_ref[pr] = jnp.dot(u_pair, wre_ref[pr], preferred_element_type=F32)
        xim_ref[pr] = jnp.dot(u_pair, wim_ref[pr], preferred_element_type=F32)

    def scan(pr):
        sr, si = stre_ref[pr], stim_ref[pr]
        ar = are_ref[:, pr * LANES:(pr + 1) * LANES]
        ai = aim_ref[:, pr * LANES:(pr + 1) * LANES]
        for c in range(n_chunks):
            r = pl.ds(c, bsz, stride=n_chunks)
            xr = xre_ref[pr, r, :]
            xi = xim_ref[pr, r, :]
            xre_ref[pr, r, :] = sr
            xim_ref[pr, r, :] = si
            sr, si = ar * sr - ai * si + xr, ar * si + ai * sr + xi
        stre_ref[pr], stim_ref[pr] = sr, si

    for g in range(gb):
        if g % 2 == 0:
            scan(g // 2)
        u = u_rows(g)
        y = jnp.dot(u, m_ref[g], preferred_element_type=F32)
        y = y + jnp.dot(xre_ref[g // 2].astype(BF16), vre_ref[g], preferred_element_type=F32)
        y = y + jnp.dot(xim_ref[g // 2].astype(BF16), vim_ref[g], preferred_element_type=F32)
        y = _gelu_tanh(y + d_ref[g] * u.astype(F32))
        gate = jnp.dot(y.astype(BF16), glu_ref[g], preferred_element_type=F32)
        y = y * jax.nn.sigmoid(gate)
        for t8 in range(width // LANES):
            yt_ref[g, t8] = y[:, t8 * LANES:(t8 + 1) * LANES]

    for b in range(bsz):
        for t8 in range(width // LANES):
            zs = [yt_ref[g, t8, b * n_chunks:(b + 1) * n_chunks, :] for g in range(gb)]
            for tt, rows_t in enumerate(_transpose_runs(zs)):
                y_ref[b, pl.ds(t8 * gb + tt, n_chunks, stride=S5_T), :] = rows_t


def _s5_params(a_re, a_im, log_dt, b_re, b_im, c_re, c_im, d_skip, w_glu):
    depth, g, p, h = b_re.shape
    t_sub = S5_T
    f = lambda v: v.astype(F32)
    tr = lambda v: jnp.swapaxes(f(v), -1, -2)
    lam_re = jnp.minimum(f(a_re), -1e-4)
    lam_im = f(a_im)
    dt = jnp.exp(f(log_dt))[..., None]
    rho, theta = lam_re * dt, lam_im * dt

    abar_re = jnp.exp(rho) * jnp.cos(theta)
    abar_im = jnp.exp(rho) * jnp.sin(theta)
    inv = 1.0 / (lam_re * lam_re + lam_im * lam_im)
    coef_re = (((abar_re - 1.0) * lam_re + abar_im * lam_im) * inv)[:, :, None, :]
    coef_im = ((abar_im * lam_re - (abar_re - 1.0) * lam_im) * inv)[:, :, None, :]
    bbar_re = coef_re * tr(b_re) - coef_im * tr(b_im)
    bbar_im = coef_re * tr(b_im) + coef_im * tr(b_re)

    tau = jnp.arange(t_sub + 1, dtype=F32)[:, None]
    pw_mag = jnp.exp(tau * rho[:, :, None, :])
    pw_re = pw_mag * jnp.cos(tau * theta[:, :, None, :])
    pw_im = pw_mag * jnp.sin(tau * theta[:, :, None, :])

    pr, pi = pw_re[:, :, :t_sub, None, :], pw_im[:, :, :t_sub, None, :]
    ab_re = pr * bbar_re[:, :, None] - pi * bbar_im[:, :, None]
    ab_im = pr * bbar_im[:, :, None] + pi * bbar_re[:, :, None]

    kern = (jnp.einsum('dgop,dgthp->dghto', f(c_re), ab_re)
            - jnp.einsum('dgop,dgthp->dghto', f(c_im), ab_im)).reshape(depth, g, h, t_sub * h)

    def pair_block_diag(w):
        w = jnp.flip(w, axis=2).reshape(depth, g // 2, 2, t_sub * h, p)
        return jnp.einsum('dqirc,ij->dqirjc', w, jnp.eye(2, dtype=F32)).reshape(
            depth, g // 2, 2 * t_sub * h, 2 * p).astype(BF16)

    w_re, w_im = pair_block_diag(ab_re), pair_block_diag(ab_im)

    p1_re, p1_im = tr(pw_re[:, :, 1:])[..., None], tr(pw_im[:, :, 1:])[..., None]
    ct_re, ct_im = tr(c_re)[:, :, :, None, :], tr(c_im)[:, :, :, None, :]

    def pair_rows(v):
        v = v.reshape(depth, g // 2, 2, p, t_sub * h)
        return jnp.einsum('dqipc,ij->dqijpc', v, jnp.eye(2, dtype=F32)).reshape(
            depth, g, 2 * p, t_sub * h).astype(BF16)

    v_re = pair_rows(ct_re * p1_re - ct_im * p1_im)
    v_im = pair_rows(-(ct_re * p1_im + ct_im * p1_re))

    at_re = jnp.broadcast_to(pw_re[:, :, t_sub].reshape(depth, 1, g * p), (depth, SUBLANES, g * p))
    at_im = jnp.broadcast_to(pw_im[:, :, t_sub].reshape(depth, 1, g * p), (depth, SUBLANES, g * p))

    d_t = jnp.tile(f(d_skip)[:, :, None, :], (1, 1, t_sub, 1)).reshape(depth, g, 1, t_sub * h)
    glu0 = jnp.pad(f(w_glu), ((0, 0), (0, 0), (0, 0), (0, (t_sub - 1) * h)))
    return kern, w_re, w_im, v_re, v_im, at_re, at_im, d_t, glu0


def _s5(layer, u_t, params):
    g, bsz, n_chunks, width = u_t.shape
    kern, w_re, w_im, v_re, v_im, at_re, at_im, d_t, glu0 = params
    gb = LANES // S5_RUN
    cb = S5_CBLK
    rows = bsz * cb
    n_pairs = gb // 2
    assert bsz == SUBLANES and width == 2 * LANES and S5_T == 2 * gb

    def per_group(arr, groups):
        blk = (None, groups) + arr.shape[2:]
        return pl.BlockSpec(blk, lambda i, j: (layer, i) + (0,) * (arr.ndim - 2))

    decay_spec = pl.BlockSpec((None, SUBLANES, n_pairs * LANES), lambda i, j: (layer, 0, i))
    return pl.pallas_call(
        _s5_kernel,
        name="s5",
        grid=(g // gb, n_chunks // cb),
        in_specs=[pl.BlockSpec((gb, bsz, cb, width), lambda i, j: (i, 0, j, 0)),
                  per_group(kern, gb), per_group(w_re, n_pairs), per_group(w_im, n_pairs),
                  per_group(v_re, gb), per_group(v_im, gb), decay_spec, decay_spec,
                  per_group(d_t, gb), per_group(glu0, gb)],
        out_specs=pl.BlockSpec((bsz, cb * S5_T, LANES), lambda i, j: (0, j, i)),
        out_shape=jax.ShapeDtypeStruct((bsz, n_chunks * S5_T, g * S5_RUN), F32),
        scratch_shapes=[
            pltpu.VMEM((n_pairs, rows, LANES), F32),
            pltpu.VMEM((n_pairs, rows, LANES), F32),
            pltpu.VMEM((n_pairs, SUBLANES, LANES), F32),
            pltpu.VMEM((n_pairs, SUBLANES, LANES), F32),
            pltpu.VMEM((gb, width // LANES, rows, LANES), F32),
            pltpu.VMEM((gb, width, width), BF16),
            pltpu.VMEM((gb, width, width), BF16),
        ],
        compiler_params=pltpu.CompilerParams(
            dimension_semantics=("arbitrary", "arbitrary"),
            vmem_limit_bytes=VMEM_LIMIT),
    )(u_t, kern, w_re, w_im, v_re, v_im, at_re, at_im, d_t, glu0)


def _attn_kernel(q_ref, k_ref, kc_ref, vt_ref, o_ref, s_ref, bmax_ref, m_ref, acc_ref, *,
                 n_q_blocks):
    pair = pl.program_id(1)
    qi = pl.program_id(2)
    tq = q_ref.shape[0]
    ts = TS_ATTN
    n_strips = tq // ts
    assert n_strips % 2 == 0
    q = q_ref[...]
    lane = lax.broadcasted_iota(jnp.int32, (ts, LANES), 1)
    k_pos = lax.broadcasted_iota(jnp.int32, (ts, ts), 0)
    q_pos = lax.broadcasted_iota(jnp.int32, (ts, ts), 1)
    nt_dims = (((1,), (1,)), ((), ()))

    chains = [(st, hh) for st in range(n_strips) for hh in range(2)]
    qs = {}
    for st, hh in chains:
        q_st = q[st * ts:(st + 1) * ts, :]
        head_lanes = (lane >= HEAD_DIM) if hh else (lane < HEAD_DIM)
        piece_lanes = (lane < CUM_PIECES * ATTN_HEADS) & (lane % ATTN_HEADS == 2 * pair + hh)
        qs[st, hh] = jnp.concatenate(
            [jnp.where(head_lanes, q_st, jnp.zeros_like(q_st)),
             jnp.where(piece_lanes, -1.0, 0.0).astype(BF16)], axis=1)

    def key_start(blk):
        return blk * ts

    def scores(blk, buf, first_strip=0):
        k0 = key_start(blk)
        kb = jnp.concatenate([k_ref[pl.ds(k0, ts), :], kc_ref[pl.ds(k0, ts), :]], axis=1)
        for c, (st, hh) in enumerate(chains):
            if st < first_strip:
                continue
            s = lax.dot_general(kb, qs[st, hh], nt_dims, preferred_element_type=F32)
            s_ref[buf, c] = s
            bmax_ref[buf, c] = jnp.max(s, axis=0, keepdims=True)

    def softmax_pv(blk, buf, first_strip=0, diag_strip=None):
        vt = vt_ref[:, pl.ds(key_start(blk), ts)]
        vt_ones = [jnp.concatenate([vt[hh * HEAD_DIM:(hh + 1) * HEAD_DIM, :], ones_rows], axis=0)
                   for hh in range(2)]
        for c, (st, hh) in enumerate(chains):
            if st < first_strip:
                continue
            s = s_ref[buf, c]
            blk_max = bmax_ref[buf, c]
            if st == diag_strip:
                s = jnp.where(k_pos <= q_pos, s, MASK_VALUE)
                blk_max = jnp.max(s, axis=0, keepdims=True)
            m = m_ref[c]
            m_new = jnp.maximum(m, blk_max)
            alpha = jnp.exp2(m - m_new)
            p = jnp.exp2(s - m_new).astype(BF16)
            pv = jnp.dot(vt_ones[hh], p, preferred_element_type=F32)
            acc_ref[c] = alpha * acc_ref[c] + pv
            m_ref[c] = m_new

    m_ref[...] = jnp.full(m_ref.shape, MASK_VALUE, F32)
    acc_ref[...] = jnp.zeros_like(acc_ref)
    ones_rows = jnp.ones((BF16_SUBLANES, ts), BF16)

    def run(n_full):
        scores(0, 0)
        for t in range(n_full // 2):
            scores(2 * t + 1, 1)
            softmax_pv(2 * t, 0)
            scores(2 * t + 2, 0)
            softmax_pv(2 * t + 1, 1)
        for d in range(n_strips):
            if d + 1 < n_strips:
                scores(n_full + d + 1, (d + 1) % 2, first_strip=d + 1)
            softmax_pv(n_full + d, d % 2, first_strip=d, diag_strip=d)

    for q_blk in range(n_q_blocks):
        pl.when(qi == q_blk)(lambda q_blk=q_blk: run(q_blk * n_strips))

    out_t = jnp.concatenate(
        [jnp.concatenate([acc_ref[c, 0:HEAD_DIM, :] / acc_ref[c, HEAD_DIM:HEAD_DIM + 1, :]
                          for c in range(st * 2, st * 2 + 2)], axis=0)
         for st in range(n_strips)], axis=1)
    o_ref[...] = out_t.T.astype(BF16)


def _attention(q, k, kc, vt):
    bsz, seq, w = q.shape
    pairs = w // LANES
    tq = T_ATTN
    ts = TS_ATTN
    n_chains = 2 * (tq // ts)
    return pl.pallas_call(
        functools.partial(_attn_kernel, n_q_blocks=seq // tq),
        name="fox_attn",
        grid=(bsz, pairs, seq // tq),
        in_specs=[
            pl.BlockSpec((None, tq, LANES), lambda b, p, i: (b, i, p)),
            pl.BlockSpec((None, seq, LANES), lambda b, p, i: (b, 0, p)),
            pl.BlockSpec((None, seq, LANES), lambda b, p, i: (b, 0, 0)),
            pl.BlockSpec((None, LANES, seq), lambda b, p, i: (b, p, 0)),
        ],
        out_specs=pl.BlockSpec((None, tq, LANES), lambda b, p, i: (b, i, p)),
        out_shape=jax.ShapeDtypeStruct(q.shape, BF16),
        scratch_shapes=[
            pltpu.VMEM((2, n_chains, ts, ts), F32),
            pltpu.VMEM((2, n_chains, 1, ts), F32),
            pltpu.VMEM((n_chains, 1, ts), F32),
            pltpu.VMEM((n_chains, HEAD_DIM + BF16_SUBLANES, ts), F32),
        ],
        compiler_params=pltpu.CompilerParams(
            dimension_semantics=("arbitrary", "arbitrary", "arbitrary"),
            vmem_limit_bytes=VMEM_LIMIT),
    )(q, k, kc, vt)


def _merge_kernel(x_ref, ya_ref, yb_ref, yc_ref, gpre_ref, wg_ref, bg_ref, wbr_ref, wout_ref,
                  gpost_ref, o_ref):
    x = x_ref[...]
    d = x.shape[1]
    hn = _rms(x, gpre_ref[...]).astype(BF16)
    merged = None
    for i, y_ref in enumerate((ya_ref, yb_ref, yc_ref)):
        cols = slice(i * d, (i + 1) * d)
        gate = jax.nn.sigmoid(
            jnp.dot(hn, wg_ref[:, cols], preferred_element_type=F32) + bg_ref[:, cols])
        br = jnp.dot(y_ref[...].astype(BF16), wbr_ref[i * BRANCH_WIDTH:(i + 1) * BRANCH_WIDTH, :],
                     preferred_element_type=F32)
        merged = gate * br if merged is None else merged + gate * br
    mix = jnp.dot(merged.astype(BF16), wout_ref[...], preferred_element_type=F32)
    o_ref[...] = x + _rms(mix, gpost_ref[...])


def _merge(layer, x, ya, yb, yc, g_pre, w_g, b_g, w_br, w_out, g_post):
    n, d = x.shape
    tm = TM_PROJ
    row_blk = lambda width: pl.BlockSpec((tm, width), lambda i: (i, 0))
    consts = (g_pre, w_g, b_g, w_br, w_out, g_post)
    return pl.pallas_call(
        _merge_kernel,
        name="merge",
        grid=(n // tm,),
        in_specs=[row_blk(d)] + [row_blk(BRANCH_WIDTH)] * 3 + [_layer_spec(c, layer) for c in consts],
        out_specs=row_blk(d),
        out_shape=jax.ShapeDtypeStruct(x.shape, F32),
        compiler_params=pltpu.CompilerParams(
            dimension_semantics=("arbitrary",),
            vmem_limit_bytes=VMEM_LIMIT),
    )(x, ya, yb, yc, *consts)


def _mlp_kernel(x_ref, gpre_ref, w1_ref, w2_ref, gpost_ref, o_ref):
    x = x_ref[...]
    d = x.shape[1]
    hn = _rms(x, gpre_ref[...]).astype(BF16)
    acc = None
    for c in range(w1_ref.shape[1] // d):
        cols = slice(c * d, (c + 1) * d)
        h1 = jnp.dot(hn, w1_ref[:, cols], preferred_element_type=F32)
        h1 = jnp.square(jnp.maximum(h1, 0.0)).astype(BF16)
        part = jnp.dot(h1, w2_ref[cols, :], preferred_element_type=F32)
        acc = part if acc is None else acc + part
    o_ref[...] = x + _rms(acc, gpost_ref[...])


def _mlp(layer, x, g_pre, w1, w2, g_post):
    n, d = x.shape
    tm = TM_PROJ
    blk = pl.BlockSpec((tm, d), lambda i: (i, 0))
    consts = (g_pre, w1, w2, g_post)
    return pl.pallas_call(
        _mlp_kernel,
        name="mlp",
        grid=(n // tm,),
        in_specs=[blk] + [_layer_spec(c, layer) for c in consts],
        out_specs=blk,
        out_shape=jax.ShapeDtypeStruct(x.shape, F32),
        compiler_params=pltpu.CompilerParams(
            dimension_semantics=("arbitrary",),
            vmem_limit_bytes=VMEM_LIMIT),
    )(x, *consts)


def kernel(x, g_pre_mix, w_in, b_gate, s5_a_re, s5_a_im, s5_log_dt, s5_b_re, s5_b_im, s5_c_re,
           s5_c_im, s5_d, s5_w_glu, conv_w, fox_b_f, w_branch, w_out, g_post_mix, g_pre_mlp,
           w_ff1, w_ff2, g_post_mlp):
    bsz, seq, d = x.shape
    depth = g_pre_mix.shape[0]
    n = bsz * seq

    n_act = 6 * BRANCH_WIDTH
    n_v = n_act + BRANCH_WIDTH
    w_in_b = w_in.astype(BF16)
    wv_t = jnp.swapaxes(w_in_b[:, :, n_act:n_v], 1, 2)
    wf_t = jnp.swapaxes(w_in_b[:, :, n_v:n_v + ATTN_HEADS], 1, 2)
    w_g = w_in_b[:, :, n_v + ATTN_HEADS:]
    w_br, w_o = w_branch.astype(BF16), w_out.astype(BF16)
    w1, w2 = w_ff1.astype(BF16), w_ff2.astype(BF16)
    row = lambda v: v.astype(F32).reshape(depth, 1, -1)
    g_mix, g_post, g_mlp, g_post2, b_g = map(
        row, (g_pre_mix, g_post_mix, g_pre_mlp, g_post_mlp, b_gate))
    b_f = fox_b_f.astype(F32).reshape(depth, ATTN_HEADS, 1)
    cw = conv_w.astype(F32)
    s5_params = _s5_params(s5_a_re, s5_a_im, s5_log_dt, s5_b_re, s5_b_im, s5_c_re, s5_c_im,
                           s5_d, s5_w_glu)

    flat = lambda t: t.reshape(n, t.shape[-1])
    for layer in range(depth):
        ut, yb, q, k, vt, kc = _in_proj(layer, x, g_mix, w_in_b, wv_t, wf_t, b_f, cw)
        ya = _s5(layer, ut, s5_params)
        yc = _attention(q, k, kc, vt)
        x1 = _merge(layer, flat(x), flat(ya), flat(yb), flat(yc), g_mix, w_g, b_g, w_br, w_o,
                    g_post)
        x = _mlp(layer, x1, g_mlp, w1, w2, g_post2).reshape(bsz, seq, d)
    return x
```

```python
import functools
import math

import jax
import jax.numpy as jnp
from jax import lax
from jax.experimental import pallas as pl
from jax.experimental.pallas import tpu as pltpu

F32 = jnp.float32
BF16 = jnp.bfloat16

EPS = 1e-6
MASK_VALUE = -1e30

ATTN_HEADS = 8
HEAD_DIM = 64
CONV_K = 3
BRANCH_WIDTH = 512

LANES = 128
SUBLANES = 8
BF16_SUBLANES = 16
VMEM_LIMIT = 56 * 1024 * 1024

TM_PROJ = 1024
TM_FUSED = 512
S5_RUN = 16
S5_T = 16
S5_CBLK = 64
T_ATTN = 1024
TS_ATTN = 256
LOG2E = math.log2(math.e)
CUM_PIECES = 3


def _rms(x, g):
    return x * lax.rsqrt(jnp.mean(x * x, axis=-1, keepdims=True) + EPS) * g


def _transpose_runs(arrs):
    n = len(arrs)
    assert n * S5_RUN == LANES
    run = lax.broadcasted_iota(jnp.int32, arrs[0].shape, 1) // S5_RUN
    d = n // 2
    while d:
        low = (run & d) == 0
        nxt = list(arrs)
        for i in range(n):
            if not i & d:
                a, b = arrs[i], arrs[i + d]
                nxt[i] = jnp.where(low, a, pltpu.roll(b, S5_RUN * d, 1))
                nxt[i + d] = jnp.where(low, pltpu.roll(a, LANES - S5_RUN * d, 1), b)
        arrs = nxt
        d //= 2
    return arrs


def _cumsum_lanes(x):
    n = x.shape[-1]
    lane = lax.broadcasted_iota(jnp.int32, x.shape, x.ndim - 1)
    shift = 1
    while shift < n:
        x = x + jnp.where(lane >= shift, pltpu.roll(x, shift, x.ndim - 1), 0.0)
        shift *= 2
    return x


def _layer_spec(arr, layer, block=None):
    block = arr.shape[1:] if block is None else block
    return pl.BlockSpec((None,) + tuple(block), lambda *_: (layer,) + (0,) * len(block),
                        pipeline_mode=pl.Buffered(1))


def _inproj_kernel(x_ref, g_ref, w_ref, wvt_ref, wft_ref, bf_ref, cw_ref,
                   ut_ref, yb_ref, q_ref, k_ref, vt_ref, kc_ref,
                   vtail_ref, carry_ref, uscr_ref):
    j = pl.program_id(1)
    tm = x_ref.shape[0]
    w = BRANCH_WIDTH

    @pl.when(j == 0)
    def _():
        vtail_ref[...] = jnp.zeros_like(vtail_ref)
        carry_ref[...] = jnp.zeros_like(carry_ref)

    hn = _rms(x_ref[...], g_ref[...]).astype(BF16)

    def proj(c):
        return jnp.dot(hn, w_ref[:, c * w:(c + 1) * w], preferred_element_type=F32)


    nt_dims = (((1,), (1,)), ((), ()))
    fl = lax.dot_general(wft_ref[...], hn, nt_dims, preferred_element_type=F32)
    z = fl + bf_ref[...]
    log_f = (jnp.minimum(z, 0.0) - jnp.log1p(jnp.exp(-jnp.abs(z)))) * LOG2E
    cum = _cumsum_lanes(log_f) + carry_ref[:, 0:1]
    carry_ref[...] = jnp.broadcast_to(cum[:, tm - 1:tm], carry_ref.shape)
    hi = cum.astype(BF16).astype(F32)
    mid = (cum - hi).astype(BF16).astype(F32)
    lo = cum - hi - mid
    pieces = jnp.concatenate(
        [hi, mid, lo, jnp.zeros((LANES - CUM_PIECES * ATTN_HEADS, tm), F32)], axis=0)
    kc_ref[...] = pieces.T.astype(BF16)

    u = proj(0)
    n_lt = w // LANES
    for lt in range(n_lt):
        uscr_ref[lt] = u[:, lt * LANES:(lt + 1) * LANES]
    n_sub = tm // S5_T
    runs_per_tile = LANES // S5_RUN
    for lt in range(n_lt):
        for t8 in range(S5_T // runs_per_tile):
            xs = [uscr_ref[lt, pl.ds(t8 * runs_per_tile + tt, n_sub, stride=S5_T), :]
                  for tt in range(runs_per_tile)]
            for gi, flat in enumerate(_transpose_runs(xs)):
                ut_ref[lt * runs_per_tile + gi, :, t8 * LANES:(t8 + 1) * LANES] = flat.astype(BF16)

    vv = proj(3) * proj(1)
    tail = vtail_ref[...]
    row = lax.broadcasted_iota(jnp.int32, vv.shape, 0)
    v1 = jnp.where(row == 0, tail[7:8, :], pltpu.roll(vv, 1, 0))
    v2 = jnp.where(row == 0, tail[6:7, :],
                   jnp.where(row == 1, tail[7:8, :], pltpu.roll(vv, 2, 0)))
    cw = cw_ref[...]
    conv = v2 * cw[0:1, :] + v1 * cw[1:2, :] + vv * cw[2:3, :]
    yb_ref[...] = (proj(2) * conv).astype(BF16)
    vtail_ref[...] = vv[tm - SUBLANES:tm, :]

    q_ref[...] = (proj(4) * (HEAD_DIM ** -0.5 * LOG2E)).astype(BF16)
    vt_ref[...] = lax.dot_general(wvt_ref[...], hn, nt_dims,
                                  preferred_element_type=F32).astype(BF16)
    k_ref[...] = proj(5).astype(BF16)


def _in_proj(layer, x, g, w_in, wv_t, wf_t, b_f, conv_w):
    bsz, seq, d = x.shape
    tm = TM_PROJ
    n_act = 6 * BRANCH_WIDTH
    n_groups = BRANCH_WIDTH // S5_RUN
    act = jax.ShapeDtypeStruct((bsz, seq, BRANCH_WIDTH), BF16)
    act_spec = pl.BlockSpec((None, tm, BRANCH_WIDTH), lambda b, j: (b, j, 0))
    return pl.pallas_call(
        _inproj_kernel,
        name="in_proj",
        grid=(bsz, seq // tm),
        in_specs=[
            pl.BlockSpec((None, tm, d), lambda b, j: (b, j, 0)),
            _layer_spec(g, layer),
            _layer_spec(w_in, layer, (d, n_act)),
            _layer_spec(wv_t, layer),
            _layer_spec(wf_t, layer),
            _layer_spec(b_f, layer),
            _layer_spec(conv_w, layer),
        ],
        out_specs=[
            pl.BlockSpec((n_groups, None, tm // S5_T, S5_T * S5_RUN), lambda b, j: (0, b, j, 0))
        ] + [act_spec] * 3 + [
            pl.BlockSpec((None, BRANCH_WIDTH, tm), lambda b, j: (b, 0, j)),
            pl.BlockSpec((None, tm, LANES), lambda b, j: (b, j, 0))],
        out_shape=[
            jax.ShapeDtypeStruct((n_groups, bsz, seq // S5_T, S5_T * S5_RUN), BF16)
        ] + [act] * 3 + [
            jax.ShapeDtypeStruct((bsz, BRANCH_WIDTH, seq), BF16),
            jax.ShapeDtypeStruct((bsz, seq, LANES), BF16)],
        scratch_shapes=[pltpu.VMEM((SUBLANES, BRANCH_WIDTH), F32),
                        pltpu.VMEM((ATTN_HEADS, LANES), F32),
                        pltpu.VMEM((BRANCH_WIDTH // LANES, tm, LANES), F32)],
        compiler_params=pltpu.CompilerParams(
            dimension_semantics=("arbitrary", "arbitrary"),
            vmem_limit_bytes=VMEM_LIMIT),
    )(x, g, w_in, wv_t, wf_t, b_f, conv_w)


def _gelu_tanh(x):
    c = math.sqrt(2.0 / math.pi)
    return 0.5 * x * (1.0 + jnp.tanh(c * (x + 0.044715 * (x * x * x))))


def _s5_kernel(u_ref, kern_ref, wre_ref, wim_ref, vre_ref, vim_ref, are_ref, aim_ref, d_ref,
               glu0_ref, y_ref, xre_ref, xim_ref, stre_ref, stim_ref, yt_ref, m_ref, glu_ref):
    j = pl.program_id(1)
    gb, bsz, n_chunks, width = u_ref.shape
    n_pairs = gb // 2
    rows = bsz * n_chunks

    @pl.when(j == 0)
    def _():
        stre_ref[...] = jnp.zeros_like(stre_ref)
        stim_ref[...] = jnp.zeros_like(stim_ref)
        lane = lax.broadcasted_iota(jnp.int32, (S5_RUN, width), 1)
        for g in range(gb):
            k_slab = kern_ref[g]
            w_slab = glu0_ref[g]
            for jj in range(S5_T):
                rows_j = slice(jj * S5_RUN, (jj + 1) * S5_RUN)
                k_j = pltpu.roll(k_slab, jj * S5_RUN, 1) if jj else k_slab
                w_j = pltpu.roll(w_slab, jj * S5_RUN, 1) if jj else w_slab
                m_ref[g, rows_j, :] = jnp.where(lane >= jj * S5_RUN, k_j, 0.0).astype(BF16)
                glu_ref[g, rows_j, :] = w_j.astype(BF16)

    def u_rows(g):
        return u_ref[g].reshape(rows, width)

    for pr in range(n_pairs):
        u_pair = jnp.concatenate([u_rows(2 * pr), u_rows(2 * pr + 1)], axis=1)
        xre_ref[pr] = jnp.dot(u_pair, wre_ref[pr], preferred_element_type=F32)
        xim_ref[pr] = jnp.dot(u_pair, wim_ref[pr], preferred_element_type=F32)

    def scan(pr):
        sr, si = stre_ref[pr], stim_ref[pr]
        ar = are_ref[:, pr * LANES:(pr + 1) * LANES]
        ai = aim_ref[:, pr * LANES:(pr + 1) * LANES]
        for c in range(n_chunks):
            r = pl.ds(c, bsz, stride=n_chunks)
            xr = xre_ref[pr, r, :]
            xi = xim_ref[pr, r, :]
            xre_ref[pr, r, :] = sr
            xim_ref[pr, r, :] = si
            sr, si = ar * sr - ai * si + xr, ar * si + ai * sr + xi
        stre_ref[pr], stim_ref[pr] = sr, si

    for g in range(gb):
        if g % 2 == 0:
            scan(g // 2)
        u = u_rows(g)
        y = jnp.dot(u, m_ref[g], preferred_element_type=F32)
        y = y + jnp.dot(xre_ref[g // 2].astype(BF16), vre_ref[g], preferred_element_type=F32)
        y = y + jnp.dot(xim_ref[g // 2].astype(BF16), vim_ref[g], preferred_element_type=F32)
        y = _gelu_tanh(y + d_ref[g] * u.astype(F32))
        gate = jnp.dot(y.astype(BF16), glu_ref[g], preferred_element_type=F32)
        y = y * jax.nn.sigmoid(gate)
        for t8 in range(width // LANES):
            yt_ref[g, t8] = y[:, t8 * LANES:(t8 + 1) * LANES]

    for b in range(bsz):
        for t8 in range(width // LANES):
            zs = [yt_ref[g, t8, b * n_chunks:(b + 1) * n_chunks, :] for g in range(gb)]
            for tt, rows_t in enumerate(_transpose_runs(zs)):
                y_ref[b, pl.ds(t8 * gb + tt, n_chunks, stride=S5_T), :] = rows_t


def _s5_params(a_re, a_im, log_dt, b_re, b_im, c_re, c_im, d_skip, w_glu):
    depth, g, p, h = b_re.shape
    t_sub = S5_T
    f = lambda v: v.astype(F32)
    tr = lambda v: jnp.swapaxes(f(v), -1, -2)
    lam_re = jnp.minimum(f(a_re), -1e-4)
    lam_im = f(a_im)
    dt = jnp.exp(f(log_dt))[..., None]
    rho, theta = lam_re * dt, lam_im * dt

    abar_re = jnp.exp(rho) * jnp.cos(theta)
    abar_im = jnp.exp(rho) * jnp.sin(theta)
    inv = 1.0 / (lam_re * lam_re + lam_im * lam_im)
    coef_re = (((abar_re - 1.0) * lam_re + abar_im * lam_im) * inv)[:, :, None, :]
    coef_im = ((abar_im * lam_re - (abar_re - 1.0) * lam_im) * inv)[:, :, None, :]
    bbar_re = coef_re * tr(b_re) - coef_im * tr(b_im)
    bbar_im = coef_re * tr(b_im) + coef_im * tr(b_re)

    tau = jnp.arange(t_sub + 1, dtype=F32)[:, None]
    pw_mag = jnp.exp(tau * rho[:, :, None, :])
    pw_re = pw_mag * jnp.cos(tau * theta[:, :, None, :])
    pw_im = pw_mag * jnp.sin(tau * theta[:, :, None, :])

    pr, pi = pw_re[:, :, :t_sub, None, :], pw_im[:, :, :t_sub, None, :]
    ab_re = pr * bbar_re[:, :, None] - pi * bbar_im[:, :, None]
    ab_im = pr * bbar_im[:, :, None] + pi * bbar_re[:, :, None]

    kern = (jnp.einsum('dgop,dgthp->dghto', f(c_re), ab_re)
            - jnp.einsum('dgop,dgthp->dghto', f(c_im), ab_im)).reshape(depth, g, h, t_sub * h)

    def pair_block_diag(w):
        w = jnp.flip(w, axis=2).reshape(depth, g // 2, 2, t_sub * h, p)
        return jnp.einsum('dqirc,ij->dqirjc', w, jnp.eye(2, dtype=F32)).reshape(
            depth, g // 2, 2 * t_sub * h, 2 * p).astype(BF16)

    w_re, w_im = pair_block_diag(ab_re), pair_block_diag(ab_im)

    p1_re, p1_im = tr(pw_re[:, :, 1:])[..., None], tr(pw_im[:, :, 1:])[..., None]
    ct_re, ct_im = tr(c_re)[:, :, :, None, :], tr(c_im)[:, :, :, None, :]

    def pair_rows(v):
        v = v.reshape(depth, g // 2, 2, p, t_sub * h)
        return jnp.einsum('dqipc,ij->dqijpc', v, jnp.eye(2, dtype=F32)).reshape(
            depth, g, 2 * p, t_sub * h).astype(BF16)

    v_re = pair_rows(ct_re * p1_re - ct_im * p1_im)
    v_im = pair_rows(-(ct_re * p1_im + ct_im * p1_re))

    at_re = jnp.broadcast_to(pw_re[:, :, t_sub].reshape(depth, 1, g * p), (depth, SUBLANES, g * p))
    at_im = jnp.broadcast_to(pw_im[:, :, t_sub].reshape(depth, 1, g * p), (depth, SUBLANES, g * p))

    d_t = jnp.tile(f(d_skip)[:, :, None, :], (1, 1, t_sub, 1)).reshape(depth, g, 1, t_sub * h)
    glu0 = jnp.pad(f(w_glu), ((0, 0), (0, 0), (0, 0), (0, (t_sub - 1) * h)))
    return kern, w_re, w_im, v_re, v_im, at_re, at_im, d_t, glu0


def _s5(layer, u_t, params):
    g, bsz, n_chunks, width = u_t.shape
    kern, w_re, w_im, v_re, v_im, at_re, at_im, d_t, glu0 = params
    gb = LANES // S5_RUN
    cb = S5_CBLK
    rows = bsz * cb
    n_pairs = gb // 2
    assert bsz == SUBLANES and width == 2 * LANES and S5_T == 2 * gb

    def per_group(arr, groups):
        blk = (None, groups) + arr.shape[2:]
        return pl.BlockSpec(blk, lambda i, j: (layer, i) + (0,) * (arr.ndim - 2))

    decay_spec = pl.BlockSpec((None, SUBLANES, n_pairs * LANES), lambda i, j: (layer, 0, i))
    return pl.pallas_call(
        _s5_kernel,
        name="s5",
        grid=(g // gb, n_chunks // cb),
        in_specs=[pl.BlockSpec((gb, bsz, cb, width), lambda i, j: (i, 0, j, 0)),
                  per_group(kern, gb), per_group(w_re, n_pairs), per_group(w_im, n_pairs),
                  per_group(v_re, gb), per_group(v_im, gb), decay_spec, decay_spec,
                  per_group(d_t, gb), per_group(glu0, gb)],
        out_specs=pl.BlockSpec((bsz, cb * S5_T, LANES), lambda i, j: (0, j, i)),
        out_shape=jax.ShapeDtypeStruct((bsz, n_chunks * S5_T, g * S5_RUN), F32),
        scratch_shapes=[
            pltpu.VMEM((n_pairs, rows, LANES), F32),
            pltpu.VMEM((n_pairs, rows, LANES), F32),
            pltpu.VMEM((n_pairs, SUBLANES, LANES), F32),
            pltpu.VMEM((n_pairs, SUBLANES, LANES), F32),
            pltpu.VMEM((gb, width // LANES, rows, LANES), F32),
            pltpu.VMEM((gb, width, width), BF16),
            pltpu.VMEM((gb, width, width), BF16),
        ],
        compiler_params=pltpu.CompilerParams(
            dimension_semantics=("arbitrary", "arbitrary"),
            vmem_limit_bytes=VMEM_LIMIT),
    )(u_t, kern, w_re, w_im, v_re, v_im, at_re, at_im, d_t, glu0)


def _attn_kernel(q_ref, k_ref, kc_ref, vt_ref, o_ref, s_ref, bmax_ref, m_ref, acc_ref, *,
                 n_q_blocks):
    pair = pl.program_id(1)
    qi = pl.program_id(2)
    tq = q_ref.shape[0]
    ts = TS_ATTN
    n_strips = tq // ts
    assert n_strips % 2 == 0
    q = q_ref[...]
    lane = lax.broadcasted_iota(jnp.int32, (ts, LANES), 1)
    k_pos = lax.broadcasted_iota(jnp.int32, (ts, ts), 0)
    q_pos = lax.broadcasted_iota(jnp.int32, (ts, ts), 1)
    nt_dims = (((1,), (1,)), ((), ()))

    chains = [(st, hh) for st in range(n_strips) for hh in range(2)]
    qs = {}
    for st, hh in chains:
        q_st = q[st * ts:(st + 1) * ts, :]
        head_lanes = (lane >= HEAD_DIM) if hh else (lane < HEAD_DIM)
        piece_lanes = (lane < CUM_PIECES * ATTN_HEADS) & (lane % ATTN_HEADS == 2 * pair + hh)
        qs[st, hh] = jnp.concatenate(
            [jnp.where(head_lanes, q_st, jnp.zeros_like(q_st)),
             jnp.where(piece_lanes, -1.0, 0.0).astype(BF16)], axis=1)

    def key_start(blk):
        return blk * ts

    def scores(blk, buf, first_strip=0):
        k0 = key_start(blk)
        kb = jnp.concatenate([k_ref[pl.ds(k0, ts), :], kc_ref[pl.ds(k0, ts), :]], axis=1)
        for c, (st, hh) in enumerate(chains):
            if st < first_strip:
                continue
            s = lax.dot_general(kb, qs[st, hh], nt_dims, preferred_element_type=F32)
            s_ref[buf, c] = s
            bmax_ref[buf, c] = jnp.max(s, axis=0, keepdims=True)

    def softmax_pv(blk, buf, first_strip=0, diag_strip=None):
        vt = vt_ref[:, pl.ds(key_start(blk), ts)]
        vt_ones = [jnp.concatenate([vt[hh * HEAD_DIM:(hh + 1) * HEAD_DIM, :], ones_rows], axis=0)
                   for hh in range(2)]
        for c, (st, hh) in enumerate(chains):
            if st < first_strip:
                continue
            s = s_ref[buf, c]
            blk_max = bmax_ref[buf, c]
            if st == diag_strip:
                s = jnp.where(k_pos <= q_pos, s, MASK_VALUE)
                blk_max = jnp.max(s, axis=0, keepdims=True)
            m = m_ref[c]
            m_new = jnp.maximum(m, blk_max)
            alpha = jnp.exp2(m - m_new)
            p = jnp.exp2(s - m_new).astype(BF16)
            pv = jnp.dot(vt_ones[hh], p, preferred_element_type=F32)
            acc_ref[c] = alpha * acc_ref[c] + pv
            m_ref[c] = m_new

    m_ref[...] = jnp.full(m_ref.shape, MASK_VALUE, F32)
    acc_ref[...] = jnp.zeros_like(acc_ref)
    ones_rows = jnp.ones((BF16_SUBLANES, ts), BF16)

    def run(n_full):
        scores(0, 0)
        for t in range(n_full // 2):
            scores(2 * t + 1, 1)
            softmax_pv(2 * t, 0)
            scores(2 * t + 2, 0)
            softmax_pv(2 * t + 1, 1)
        for d in range(n_strips):
            if d + 1 < n_strips:
                scores(n_full + d + 1, (d + 1) % 2, first_strip=d + 1)
            softmax_pv(n_full + d, d % 2, first_strip=d, diag_strip=d)

    for q_blk in range(n_q_blocks):
        pl.when(qi == q_blk)(lambda q_blk=q_blk: run(q_blk * n_strips))

    out_t = jnp.concatenate(
        [jnp.concatenate([acc_ref[c, 0:HEAD_DIM, :] / acc_ref[c, HEAD_DIM:HEAD_DIM + 1, :]
                          for c in range(st * 2, st * 2 + 2)], axis=0)
         for st in range(n_strips)], axis=1)
    o_ref[...] = out_t.T.astype(BF16)


def _attention(q, k, kc, vt):
    bsz, seq, w = q.shape
    pairs = w // LANES
    tq = T_ATTN
    ts = TS_ATTN
    n_chains = 2 * (tq // ts)
    return pl.pallas_call(
        functools.partial(_attn_kernel, n_q_blocks=seq // tq),
        name="fox_attn",
        grid=(bsz, pairs, seq // tq),
        in_specs=[
            pl.BlockSpec((None, tq, LANES), lambda b, p, i: (b, i, p)),
            pl.BlockSpec((None, seq, LANES), lambda b, p, i: (b, 0, p)),
            pl.BlockSpec((None, seq, LANES), lambda b, p, i: (b, 0, 0)),
            pl.BlockSpec((None, LANES, seq), lambda b, p, i: (b, p, 0)),
        ],
        out_specs=pl.BlockSpec((None, tq, LANES), lambda b, p, i: (b, i, p)),
        out_shape=jax.ShapeDtypeStruct(q.shape, BF16),
        scratch_shapes=[
            pltpu.VMEM((2, n_chains, ts, ts), F32),
            pltpu.VMEM((2, n_chains, 1, ts), F32),
            pltpu.VMEM((n_chains, 1, ts), F32),
            pltpu.VMEM((n_chains, HEAD_DIM + BF16_SUBLANES, ts), F32),
        ],
        compiler_params=pltpu.CompilerParams(
            dimension_semantics=("arbitrary", "arbitrary", "arbitrary"),
            vmem_limit_bytes=VMEM_LIMIT),
    )(q, k, kc, vt)


def _merge_mlp_kernel(x_ref, ya_ref, yb_ref, yc_ref, gpre_ref, wg_ref, bg_ref, wbr_ref, wout_ref,
                      gpost_ref, gpre2_ref, w1_ref, w2_ref, gpost2_ref, o_ref):
    x = x_ref[...]
    d = x.shape[1]
    hn = _rms(x, gpre_ref[...]).astype(BF16)
    merged = None
    for i, y_ref in enumerate((ya_ref, yb_ref, yc_ref)):
        cols = slice(i * d, (i + 1) * d)
        gate = jax.nn.sigmoid(
            jnp.dot(hn, wg_ref[:, cols], preferred_element_type=F32) + bg_ref[:, cols])
        br = jnp.dot(y_ref[...].astype(BF16), wbr_ref[i * BRANCH_WIDTH:(i + 1) * BRANCH_WIDTH, :],
                     preferred_element_type=F32)
        merged = gate * br if merged is None else merged + gate * br
    mix = jnp.dot(merged.astype(BF16), wout_ref[...], preferred_element_type=F32)
    x1 = x + _rms(mix, gpost_ref[...])

    hn = _rms(x1, gpre2_ref[...]).astype(BF16)
    acc = None
    for c in range(w1_ref.shape[1] // d):
        cols = slice(c * d, (c + 1) * d)
        h1 = jnp.dot(hn, w1_ref[:, cols], preferred_element_type=F32)
        h1 = jnp.square(jnp.maximum(h1, 0.0)).astype(BF16)
        part = jnp.dot(h1, w2_ref[cols, :], preferred_element_type=F32)
        acc = part if acc is None else acc + part
    o_ref[...] = x1 + _rms(acc, gpost2_ref[...])


def _merge_mlp(layer, x, ya, yb, yc, g_pre, w_g, b_g, w_br, w_out, g_post, g_pre2, w1, w2, g_post2):
    n, d = x.shape
    tm = TM_FUSED
    row_blk = lambda width: pl.BlockSpec((tm, width), lambda i: (i, 0))
    consts = (g_pre, w_g, b_g, w_br, w_out, g_post, g_pre2, w1, w2, g_post2)
    return pl.pallas_call(
        _merge_mlp_kernel,
        name="merge_mlp",
        grid=(n // tm,),
        in_specs=[row_blk(d)] + [row_blk(BRANCH_WIDTH)] * 3 + [_layer_spec(c, layer) for c in consts],
        out_specs=row_blk(d),
        out_shape=jax.ShapeDtypeStruct(x.shape, F32),
        compiler_params=pltpu.CompilerParams(
            dimension_semantics=("arbitrary",),
            vmem_limit_bytes=VMEM_LIMIT),
    )(x, ya, yb, yc, *consts)


def kernel(x, g_pre_mix, w_in, b_gate, s5_a_re, s5_a_im, s5_log_dt, s5_b_re, s5_b_im, s5_c_re,
           s5_c_im, s5_d, s5_w_glu, conv_w, fox_b_f, w_branch, w_out, g_post_mix, g_pre_mlp,
           w_ff1, w_ff2, g_post_mlp):
    bsz, seq, d = x.shape
    depth = g_pre_mix.shape[0]
    n = bsz * seq

    n_act = 6 * BRANCH_WIDTH
    n_v = n_act + BRANCH_WIDTH
    w_in_b = w_in.astype(BF16)
    wv_t = jnp.swapaxes(w_in_b[:, :, n_act:n_v], 1, 2)
    wf_t = jnp.swapaxes(w_in_b[:, :, n_v:n_v + ATTN_HEADS], 1, 2)
    w_g = w_in_b[:, :, n_v + ATTN_HEADS:]
    w_br, w_o = w_branch.astype(BF16), w_out.astype(BF16)
    w1, w2 = w_ff1.astype(BF16), w_ff2.astype(BF16)
    row = lambda v: v.astype(F32).reshape(depth, 1, -1)
    g_mix, g_post, g_mlp, g_post2, b_g = map(
        row, (g_pre_mix, g_post_mix, g_pre_mlp, g_post_mlp, b_gate))
    b_f = fox_b_f.astype(F32).reshape(depth, ATTN_HEADS, 1)
    cw = conv_w.astype(F32)
    s5_params = _s5_params(s5_a_re, s5_a_im, s5_log_dt, s5_b_re, s5_b_im, s5_c_re, s5_c_im,
                           s5_d, s5_w_glu)

    flat = lambda t: t.reshape(n, t.shape[-1])
    for layer in range(depth):
        ut, yb, q, k, vt, kc = _in_proj(layer, x, g_mix, w_in_b, wv_t, wf_t, b_f, cw)
        ya = _s5(layer, ut, s5_params)
        yc = _attention(q, k, kc, vt)
        x = _merge_mlp(layer, flat(x), flat(ya), flat(yb), flat(yc), g_mix, w_g, b_g, w_br, w_o,
                       g_post, g_mlp, w1, w2, g_post2).reshape(bsz, seq, d)
    return x
```
